```python
import jax, jax.numpy as jnp
from jax import lax
import numpy as np

D_MODEL = 2048
BATCH = 1
SEQ = 16384
DEPTH = 2
DEC_BATCH = 32
DEC_SEQ = 16
PAST_LEN = 2048

CHUNK = 64
N_MIXERS = 2
N_POOL_LAYERS = (DEPTH + 1) // 2
N_FOX_LAYERS = DEPTH // 2
POOL_WINDOWS = (2, 4, 8, 16)
N_POOL_GROUPS = len(POOL_WINDOWS)
POOL_GROUP = D_MODEL // N_POOL_GROUPS
POOL_STATE = max(POOL_WINDOWS) - 1
FOX_HEAD_DIM = 128
FOX_HEADS = D_MODEL // FOX_HEAD_DIM
D_FF = 4 * D_MODEL
Q_BLOCK = 128
LN_EPS = 1e-5
DN_ALPHA = (2 * DEPTH) ** 0.25
DN_BETA = (8 * DEPTH) ** -0.25
FORGET_BIAS_INIT = 3.0

kernel_name = "pool_fox_macaron_deepnorm_stream_step"


def layer_norm(x, g, b):
    xf = x.astype(jnp.float32)
    mu = jnp.mean(xf, axis=-1, keepdims=True)
    var = jnp.mean(jnp.square(xf - mu), axis=-1, keepdims=True)
    return ((xf - mu) * lax.rsqrt(var + LN_EPS) * g + b).astype(x.dtype)


def swiglu(x, w1, w3, w2):
    return (jax.nn.silu(x @ w1) * (x @ w3)) @ w2


def pool_mix(x, prev, start_pos, w_grp, scale):
    B, T, D = x.shape
    xr = jnp.concatenate([prev, x], axis=1)
    xe = xr.astype(jnp.float32)
    cs = jnp.concatenate([jnp.zeros_like(xe[:, :1]), jnp.cumsum(xe, axis=1)], axis=1)
    end = cs[:, POOL_STATE + 1:]
    pos = start_pos + jnp.arange(T)
    diffs = []
    for g, w in enumerate(POOL_WINDOWS):
        sl = slice(g * POOL_GROUP, (g + 1) * POOL_GROUP)
        win = end[:, :, sl] - cs[:, POOL_STATE + 1 - w:POOL_STATE + 1 - w + T, sl]
        cnt = jnp.minimum(pos + 1, w).astype(jnp.float32)[None, :, None]
        diffs.append(win / cnt - xe[:, POOL_STATE:, sl])
    d = jnp.stack(diffs, axis=2).astype(x.dtype)
    y = jnp.einsum('btgc,gcd->btgd', d, w_grp).reshape(B, T, D)
    return y * scale, xr[:, -POOL_STATE:]


def fox_project(x, w_in, b_f):
    B, T, D = x.shape
    p = x @ w_in
    q, k, v, fl = jnp.split(p, [D, 2 * D, 3 * D], axis=-1)
    shp = (B, T, FOX_HEADS, FOX_HEAD_DIM)
    logf = jax.nn.log_sigmoid(fl.astype(jnp.float32) + b_f.astype(jnp.float32))
    return q.reshape(shp), k.reshape(shp), v.reshape(shp), logf


def fox_attend_prompt(q, k, v, logf):
    B, S, H, DH = q.shape
    scale = DH ** -0.5
    c = jnp.cumsum(logf, axis=1)
    cT = c.transpose(0, 2, 1)
    nb = S // Q_BLOCK
    qb = q.reshape(B, nb, Q_BLOCK, H, DH).transpose(1, 0, 2, 3, 4)
    cb = cT.reshape(B, H, nb, Q_BLOCK).transpose(2, 0, 1, 3)
    kpos = jnp.arange(S)

    def block(args):
        i, qi, ci = args
        s = jnp.einsum('bqhd,bkhd->bhqk', qi, k, preferred_element_type=jnp.float32) * scale
        s = s + ci[..., None] - cT[:, :, None, :]
        qpos = i * Q_BLOCK + jnp.arange(Q_BLOCK)
        s = jnp.where(kpos[None, :] <= qpos[:, None], s, -jnp.inf)
        p = jax.nn.softmax(s, axis=-1)
        return jnp.einsum('bhqk,bkhd->bqhd', p.astype(v.dtype), v)

    o = lax.map(block, (jnp.arange(nb), qb, cb))
    return o.transpose(1, 0, 2, 3, 4).reshape(B, S, H * DH)


def fox_attend_sample(q, k_new, v_new, logf_new, k_cache, v_cache, logf_cache):
    B, T, H, DH = q.shape
    P = k_cache.shape[1]
    scale = DH ** -0.5
    k = jnp.concatenate([k_cache.astype(k_new.dtype), k_new], axis=1)
    v = jnp.concatenate([v_cache.astype(v_new.dtype), v_new], axis=1)
    lf = jnp.concatenate([logf_cache.astype(jnp.float32), logf_new], axis=1)
    cT = jnp.cumsum(lf, axis=1).transpose(0, 2, 1)
    s = jnp.einsum('bqhd,bkhd->bhqk', q, k, preferred_element_type=jnp.float32) * scale
    s = s + cT[:, :, P:, None] - cT[:, :, None, :]
    qpos = P + jnp.arange(T)
    kpos = jnp.arange(P + T)
    s = jnp.where(kpos[None, :] <= qpos[:, None], s, -jnp.inf)
    p = jax.nn.softmax(s, axis=-1)
    o = jnp.einsum('bhqk,bkhd->bqhd', p.astype(v.dtype), v)
    return o.reshape(B, T, H * DH)


def run_trunk(x, start_pos, pool_prev, fox_cache, ln_g, ln_b, ffn_w1, ffn_w3, ffn_w2,
              pool_w, pool_scale, fox_w_in, fox_b_f, fox_w_o):
    B = x.shape[0]
    pool_new, k_new, v_new, lf_new = [], [], [], []
    for i in range(DEPTH):
        j = i // N_MIXERS
        h = swiglu(x, ffn_w1[i, 0], ffn_w3[i, 0], ffn_w2[i, 0])
        x = layer_norm(DN_ALPHA * x + 0.5 * h, ln_g[i, 0], ln_b[i, 0])
        if i % N_MIXERS == 0:
            prev = jnp.zeros((B, POOL_STATE, D_MODEL), x.dtype) if pool_prev is None else pool_prev[j].astype(x.dtype)
            m, st = pool_mix(x, prev, start_pos, pool_w[j], pool_scale[j])
            pool_new.append(st)
        else:
            q, k, v, lf = fox_project(x, fox_w_in[j], fox_b_f[j])
            if fox_cache is None:
                o = fox_attend_prompt(q, k, v, lf)
            else:
                ck, cv, cl = fox_cache
                o = fox_attend_sample(q, k, v, lf, ck[j], cv[j], cl[j])
            m = o @ fox_w_o[j]
            k_new.append(k)
            v_new.append(v)
            lf_new.append(lf)
        x = layer_norm(DN_ALPHA * x + m, ln_g[i, 1], ln_b[i, 1])
        h = swiglu(x, ffn_w1[i, 1], ffn_w3[i, 1], ffn_w2[i, 1])
        x = layer_norm(DN_ALPHA * x + 0.5 * h, ln_g[i, 2], ln_b[i, 2])
    return x, jnp.stack(pool_new), jnp.stack(k_new), jnp.stack(v_new), jnp.stack(lf_new)


def setup_inputs(seed: int = 0) -> dict:
    key = jax.random.key(seed)
    ks = jax.random.split(key, 20)
    f32 = jnp.float32

    def nrm(k, shape, s):
        return jax.random.normal(k, shape, f32) * s

    D, H = D_MODEL, FOX_HEADS
    x_prompt = nrm(ks[0], (BATCH, SEQ, D), 1.0)
    x_sample = nrm(ks[1], (DEC_BATCH, DEC_SEQ, D), 1.0)
    state_pool = nrm(ks[2], (N_POOL_LAYERS, DEC_BATCH, POOL_STATE, D), 1.0)
    cache_fox_k = nrm(ks[3], (N_FOX_LAYERS, DEC_BATCH, PAST_LEN, H, FOX_HEAD_DIM), 1.0)
    cache_fox_v = nrm(ks[4], (N_FOX_LAYERS, DEC_BATCH, PAST_LEN, H, FOX_HEAD_DIM), 1.0)
    cache_fox_logf = jax.nn.log_sigmoid(FORGET_BIAS_INIT + nrm(ks[5], (N_FOX_LAYERS, DEC_BATCH, PAST_LEN, H), 1.0))
    ln_g = 1.0 + nrm(ks[6], (DEPTH, 3, D), 0.02)
    ln_b = nrm(ks[7], (DEPTH, 3, D), 0.02)
    ffn_w1 = nrm(ks[8], (DEPTH, 2, D, D_FF), D ** -0.5)
    ffn_w3 = nrm(ks[9], (DEPTH, 2, D, D_FF), D ** -0.5)
    ffn_w2 = nrm(ks[10], (DEPTH, 2, D_FF, D), D_FF ** -0.5 * DN_BETA)
    pool_w = nrm(ks[11], (N_POOL_LAYERS, N_POOL_GROUPS, POOL_GROUP, POOL_GROUP), POOL_GROUP ** -0.5 * DN_BETA)
    pool_scale = 1.0 + nrm(ks[12], (N_POOL_LAYERS, D), 0.02)
    w_qk = nrm(ks[13], (N_FOX_LAYERS, D, 2 * D), D ** -0.5)
    w_v = nrm(ks[14], (N_FOX_LAYERS, D, D), D ** -0.5 * DN_BETA)
    w_f = nrm(ks[15], (N_FOX_LAYERS, D, H), D ** -0.5)
    fox_w_in = jnp.concatenate([w_qk, w_v, w_f], axis=-1)
    fox_b_f = FORGET_BIAS_INIT + nrm(ks[16], (N_FOX_LAYERS, H), 0.1)
    fox_w_o = nrm(ks[17], (N_FOX_LAYERS, D, D), D ** -0.5 * DN_BETA)
    return {"x_prompt": x_prompt, "x_sample": x_sample, "state_pool": state_pool,
            "cache_fox_k": cache_fox_k, "cache_fox_v": cache_fox_v, "cache_fox_logf": cache_fox_logf,
            "ln_g": ln_g, "ln_b": ln_b, "ffn_w1": ffn_w1, "ffn_w3": ffn_w3, "ffn_w2": ffn_w2,
            "pool_w": pool_w, "pool_scale": pool_scale, "fox_w_in": fox_w_in, "fox_b_f": fox_b_f,
            "fox_w_o": fox_w_o}


def reference(x_prompt, x_sample, state_pool, cache_fox_k, cache_fox_v, cache_fox_logf,
              ln_g, ln_b, ffn_w1, ffn_w3, ffn_w2, pool_w, pool_scale, fox_w_in, fox_b_f, fox_w_o):
    y_prompt, pool_p, k_p, v_p, lf_p = run_trunk(
        x_prompt, 0, None, None, ln_g, ln_b, ffn_w1, ffn_w3, ffn_w2,
        pool_w, pool_scale, fox_w_in, fox_b_f, fox_w_o)
    past = cache_fox_k.shape[2]
    y_sample, pool_s, k_s, v_s, lf_s = run_trunk(
        x_sample, past, state_pool, (cache_fox_k, cache_fox_v, cache_fox_logf),
        ln_g, ln_b, ffn_w1, ffn_w3, ffn_w2, pool_w, pool_scale, fox_w_in, fox_b_f, fox_w_o)
    return (y_prompt, y_sample, pool_p, pool_s, k_p, v_p, lf_p, k_s, v_s, lf_s)
```

```python
import functools

import jax
import jax.numpy as jnp
from jax import lax
from jax.experimental import pallas as pl
from jax.experimental.pallas import tpu as pltpu

F32 = jnp.float32
BF16 = jnp.bfloat16

D_MODEL = 2048
D_FF = 4 * D_MODEL
DEPTH = 2
POOL_WINDOWS = (2, 4, 8, 16)
POOL_GROUP = D_MODEL // len(POOL_WINDOWS)
POOL_STATE = max(POOL_WINDOWS) - 1
HALO = POOL_STATE + 1
HEAD_DIM = 128
HEADS = D_MODEL // HEAD_DIM
LN_EPS = 1e-5
DN_ALPHA = (2 * DEPTH) ** 0.25
ATTN_SCALE = HEAD_DIM ** -0.5
NEG_BIG = -1e30
LANES = 128
VMEM_LIMIT = 56 * 1024 * 1024


def _params(semantics):
    return pltpu.CompilerParams(dimension_semantics=semantics, vmem_limit_bytes=VMEM_LIMIT)


def _layer_norm(y, g, b):
    mu = jnp.mean(y, axis=-1, keepdims=True)
    yc = y - mu
    var = jnp.mean(yc * yc, axis=-1, keepdims=True)
    return yc * lax.rsqrt(var + LN_EPS) * g + b


def _ffn_kernel(x_ref, w1_ref, w3_ref, w2_ref, g_ref, b_ref, o_ref, xb_ref, acc_ref):
    j = pl.program_id(1)

    @pl.when(j == 0)
    def _():
        xb_ref[...] = x_ref[...].astype(BF16)
        acc_ref[...] = jnp.zeros_like(acc_ref)

    xb = xb_ref[...]
    h1 = jnp.dot(xb, w1_ref[...], preferred_element_type=F32)
    h3 = jnp.dot(xb, w3_ref[...], preferred_element_type=F32)
    gate = (h1 * jax.nn.sigmoid(h1) * h3).astype(BF16)
    acc_ref[...] += jnp.dot(gate, w2_ref[...], preferred_element_type=F32)

    @pl.when(j == pl.num_programs(1) - 1)
    def _():
        y = DN_ALPHA * x_ref[...] + 0.5 * acc_ref[...]
        o_ref[...] = _layer_norm(y, g_ref[...], b_ref[...])


def _ffn_ln(x, w1, w3, w2, g, b, *, tm=512, tf=512):
    t, d = x.shape
    f = w1.shape[1]
    return pl.pallas_call(
        _ffn_kernel,
        grid=(t // tm, f // tf),
        in_specs=[
            pl.BlockSpec((tm, d), lambda i, j: (i, 0)),
            pl.BlockSpec((d, tf), lambda i, j: (0, j)),
            pl.BlockSpec((d, tf), lambda i, j: (0, j)),
            pl.BlockSpec((tf, d), lambda i, j: (j, 0)),
            pl.BlockSpec((1, d), lambda i, j: (0, 0)),
            pl.BlockSpec((1, d), lambda i, j: (0, 0)),
        ],
        out_specs=pl.BlockSpec((tm, d), lambda i, j: (i, 0)),
        out_shape=jax.ShapeDtypeStruct((t, d), F32),
        scratch_shapes=[pltpu.VMEM((tm, d), BF16), pltpu.VMEM((tm, d), F32)],
        compiler_params=_params(("parallel", "arbitrary")),
        name="ffn_ln",
    )(x, w1, w3, w2, g, b)


def _pool_kernel(x_ref, halo_ref, pw_ref, ps_ref, g_ref, b_ref, o_ref, buf_ref, *, tm, start_pos, zero_first):
    i = pl.program_id(1)
    bb = x_ref.shape[0]
    x = x_ref[...]
    halo = halo_ref[...]
    if zero_first:
        halo = jnp.where(i == 0, 0.0, halo)
    pos = start_pos + i * tm + lax.broadcasted_iota(jnp.int32, (1, tm, 1), 1)
    ys = []
    for g, w in enumerate(POOL_WINDOWS):
        cols = slice(g * POOL_GROUP, (g + 1) * POOL_GROUP)
        xg = x[:, :, cols]
        buf_ref[:, 0:HALO, :] = halo[:, :, cols]
        buf_ref[:, HALO:, :] = xg
        win = xg
        for k in range(1, w):
            win = win + buf_ref[:, HALO - k:HALO - k + tm, :]
        cnt = jnp.minimum(pos + 1, w).astype(F32)
        diff = (win / cnt - xg).reshape(bb * tm, POOL_GROUP)
        ys.append(jnp.dot(diff.astype(BF16), pw_ref[g], preferred_element_type=F32))
    y = jnp.concatenate(ys, axis=-1) * ps_ref[...]
    out = _layer_norm(DN_ALPHA * x.reshape(bb * tm, D_MODEL) + y, g_ref[...], b_ref[...])
    o_ref[...] = out.reshape(bb, tm, D_MODEL)


def _pool_ln(x, halo_src, halo_map, pw, ps, g, b, *, bb, tm, start_pos, zero_first):
    nb, t, d = x.shape
    kern = functools.partial(_pool_kernel, tm=tm, start_pos=start_pos, zero_first=zero_first)
    return pl.pallas_call(
        kern,
        grid=(nb // bb, t // tm),
        in_specs=[
            pl.BlockSpec((bb, tm, d), lambda bi, i: (bi, i, 0)),
            pl.BlockSpec((bb, HALO, d), halo_map),
            pl.BlockSpec((len(POOL_WINDOWS), POOL_GROUP, POOL_GROUP), lambda bi, i: (0, 0, 0)),
            pl.BlockSpec((1, d), lambda bi, i: (0, 0)),
            pl.BlockSpec((1, d), lambda bi, i: (0, 0)),
            pl.BlockSpec((1, d), lambda bi, i: (0, 0)),
        ],
        out_specs=pl.BlockSpec((bb, tm, d), lambda bi, i: (bi, i, 0)),
        out_shape=jax.ShapeDtypeStruct((nb, t, d), F32),
        scratch_shapes=[pltpu.VMEM((bb, HALO + tm, POOL_GROUP), F32)],
        compiler_params=_params(("parallel", "arbitrary")),
        name="pool_ln",
    )(x, halo_src, pw, ps, g, b)


def _proj_kernel(x_ref, wq_ref, wk_ref, wv_ref, wf_ref, bf_ref,
                 q_ref, k_ref, v_ref, kb_ref, vb_ref, lf_ref, xb_ref):
    j = pl.program_id(1)

    @pl.when(j == 0)
    def _():
        xb = x_ref[...].astype(BF16)
        xb_ref[...] = xb
        fl = jnp.dot(xb, wf_ref[...], preferred_element_type=F32) + bf_ref[...]
        lf_ref[...] = jnp.minimum(fl, 0.0) - jnp.log1p(jnp.exp(-jnp.abs(fl)))

    xb = xb_ref[...]
    q_ref[...] = jnp.dot(xb, wq_ref[...], preferred_element_type=F32).astype(BF16)
    k = jnp.dot(xb, wk_ref[...], preferred_element_type=F32)
    k_ref[...] = k
    kb_ref[...] = k.astype(BF16)
    v = jnp.dot(xb, wv_ref[...], preferred_element_type=F32)
    v_ref[...] = v
    vb_ref[...] = v.astype(BF16)


def _fox_proj(x, wq, wk, wv, wf, bf, *, tm=512, tn=512):
    t, d = x.shape
    row = pl.BlockSpec((tm, tn), lambda i, j: (i, j))
    wspec = pl.BlockSpec((d, tn), lambda i, j: (0, j))
    return pl.pallas_call(
        _proj_kernel,
        grid=(t // tm, d // tn),
        in_specs=[
            pl.BlockSpec((tm, d), lambda i, j: (i, 0)),
            wspec, wspec, wspec,
            pl.BlockSpec((d, LANES), lambda i, j: (0, 0)),
            pl.BlockSpec((1, LANES), lambda i, j: (0, 0)),
        ],
        out_specs=[row, row, row, row, row, pl.BlockSpec((tm, LANES), lambda i, j: (i, 0))],
        out_shape=[
            jax.ShapeDtypeStruct((t, d), BF16),
            jax.ShapeDtypeStruct((t, d), F32),
            jax.ShapeDtypeStruct((t, d), F32),
            jax.ShapeDtypeStruct((t, d), BF16),
            jax.ShapeDtypeStruct((t, d), BF16),
            jax.ShapeDtypeStruct((t, LANES), F32),
        ],
        scratch_shapes=[pltpu.VMEM((tm, d), BF16)],
        compiler_params=_params(("parallel", "arbitrary")),
        name="fox_proj",
    )(x, wq, wk, wv, wf, bf)


def _cumsum_kernel(x_ref, o_ref, carry_ref, *, tc):
    @pl.when(pl.program_id(1) == 0)
    def _():
        carry_ref[...] = jnp.zeros_like(carry_ref)

    x = x_ref[...]
    r = lax.broadcasted_iota(jnp.int32, (tc, tc), 0)
    c = lax.broadcasted_iota(jnp.int32, (tc, tc), 1)
    tri = (c <= r).astype(BF16)
    hi = x.astype(BF16)
    rem = x - hi.astype(F32)
    mid = rem.astype(BF16)
    lo = (rem - mid.astype(F32)).astype(BF16)
    cs = (jnp.dot(tri, hi, preferred_element_type=F32)
          + jnp.dot(tri, mid, preferred_element_type=F32)
          + jnp.dot(tri, lo, preferred_element_type=F32)) + carry_ref[...]
    o_ref[...] = cs
    carry_ref[...] = cs[tc - 1:tc, :]


def _cumsum_time(x, *, tc):
    nb, t, w = x.shape
    return pl.pallas_call(
        functools.partial(_cumsum_kernel, tc=tc),
        grid=(nb, t // tc),
        in_specs=[pl.BlockSpec((None, tc, w), lambda b, j: (b, j, 0))],
        out_specs=pl.BlockSpec((None, tc, w), lambda b, j: (b, j, 0)),
        out_shape=jax.ShapeDtypeStruct((nb, t, w), F32),
        scratch_shapes=[pltpu.VMEM((1, w), F32)],
        compiler_params=_params(("parallel", "arbitrary")),
        name="cumsum_time",
    )(x)


def _softmax_step(s, h, v_h, m_ref, l_ref, acc_ref):
    hs = slice(h, h + 1)
    sl = slice(h * HEAD_DIM, (h + 1) * HEAD_DIM)
    m_prev = m_ref[:, hs]
    m_new = jnp.maximum(m_prev, jnp.max(s, axis=-1, keepdims=True))
    a = jnp.exp(m_prev - m_new)
    p = jnp.exp(s - m_new)
    l_ref[:, hs] = a * l_ref[:, hs] + jnp.sum(p, axis=-1, keepdims=True)
    acc_ref[:, sl] = a * acc_ref[:, sl] + jnp.dot(p.astype(BF16), v_h, preferred_element_type=F32)
    m_ref[:, hs] = m_new


def _qk(q_h, k_h):
    return lax.dot_general(q_h, k_h, (((1,), (1,)), ((), ())), preferred_element_type=F32)


def _attn_prompt_kernel(q_ref, k_ref, v_ref, c_ref, ct_ref, o_ref, m_ref, l_ref, acc_ref, *, tq):
    i = pl.program_id(0)
    j = pl.program_id(1)

    @pl.when(j == 0)
    def _():
        m_ref[...] = jnp.full_like(m_ref, NEG_BIG)
        l_ref[...] = jnp.zeros_like(l_ref)
        acc_ref[...] = jnp.zeros_like(acc_ref)

    @pl.when(j <= i)
    def _():
        rows = i * tq + lax.broadcasted_iota(jnp.int32, (tq, tq), 0)
        cols = j * tq + lax.broadcasted_iota(jnp.int32, (tq, tq), 1)
        visible = cols <= rows
        for h in range(HEADS):
            sl = slice(h * HEAD_DIM, (h + 1) * HEAD_DIM)
            s = _qk(q_ref[:, sl], k_ref[:, sl]) * ATTN_SCALE + c_ref[:, h:h + 1] - ct_ref[h:h + 1, :]
            s = jnp.where(visible, s, NEG_BIG)
            _softmax_step(s, h, v_ref[:, sl], m_ref, l_ref, acc_ref)

    @pl.when(j == i)
    def _():
        for h in range(HEADS):
            sl = slice(h * HEAD_DIM, (h + 1) * HEAD_DIM)
            o_ref[:, sl] = (acc_ref[:, sl] / l_ref[:, h:h + 1]).astype(BF16)


def _attn_prompt(q, kb, vb, c, ct, *, tq=512):
    s, d = q.shape
    n = s // tq
    kv = pl.BlockSpec((tq, d), lambda i, j: (jnp.minimum(j, i), 0))
    return pl.pallas_call(
        functools.partial(_attn_prompt_kernel, tq=tq),
        grid=(n, n),
        in_specs=[
            pl.BlockSpec((tq, d), lambda i, j: (i, 0)),
            kv, kv,
            pl.BlockSpec((tq, LANES), lambda i, j: (i, 0)),
            pl.BlockSpec((HEADS, tq), lambda i, j: (0, jnp.minimum(j, i))),
        ],
        out_specs=pl.BlockSpec((tq, d), lambda i, j: (i, 0)),
        out_shape=jax.ShapeDtypeStruct((s, d), BF16),
        scratch_shapes=[pltpu.VMEM((tq, HEADS), F32), pltpu.VMEM((tq, HEADS), F32), pltpu.VMEM((tq, d), F32)],
        compiler_params=_params(("parallel", "arbitrary")),
        name="attn_prompt",
    )(q, kb, vb, c, ct)


def _attn_sample_kernel(q_ref, kc_ref, vc_ref, kn_ref, vn_ref, cq_ref, ctc_ref, ctn_ref, o_ref,
                        m_ref, l_ref, acc_ref):
    j = pl.program_id(1)
    tnew = q_ref.shape[0]

    @pl.when(j == 0)
    def _():
        m_ref[...] = jnp.full_like(m_ref, NEG_BIG)
        l_ref[...] = jnp.zeros_like(l_ref)
        acc_ref[...] = jnp.zeros_like(acc_ref)

    for h in range(HEADS):
        sl = slice(h * HEAD_DIM, (h + 1) * HEAD_DIM)
        s = _qk(q_ref[:, sl], kc_ref[:, sl].astype(BF16)) * ATTN_SCALE + cq_ref[:, h:h + 1] - ctc_ref[h:h + 1, :]
        _softmax_step(s, h, vc_ref[:, sl].astype(BF16), m_ref, l_ref, acc_ref)

    @pl.when(j == pl.num_programs(1) - 1)
    def _():
        rows = lax.broadcasted_iota(jnp.int32, (tnew, tnew), 0)
        cols = lax.broadcasted_iota(jnp.int32, (tnew, tnew), 1)
        visible = cols <= rows
        for h in range(HEADS):
            sl = slice(h * HEAD_DIM, (h + 1) * HEAD_DIM)
            s = _qk(q_ref[:, sl], kn_ref[:, sl].astype(BF16)) * ATTN_SCALE + cq_ref[:, h:h + 1] - ctn_ref[h:h + 1, :]
            s = jnp.where(visible, s, NEG_BIG)
            _softmax_step(s, h, vn_ref[:, sl].astype(BF16), m_ref, l_ref, acc_ref)
            o_ref[:, sl] = (acc_ref[:, sl] / l_ref[:, h:h + 1]).astype(BF16)


def _attn_sample(q, k_cache, v_cache, k_new, v_new, c_new, ct_cache, ct_new, *, tnew, tp=512):
    t, d = q.shape
    nb, past, _ = k_cache.shape
    new_rows = pl.BlockSpec((tnew, d), lambda b, j: (b, 0))
    cache = pl.BlockSpec((None, tp, d), lambda b, j: (b, j, 0))
    return pl.pallas_call(
        _attn_sample_kernel,
        grid=(nb, past // tp),
        in_specs=[
            new_rows, cache, cache, new_rows, new_rows,
            pl.BlockSpec((tnew, LANES), lambda b, j: (b, 0)),
            pl.BlockSpec((None, HEADS, tp), lambda b, j: (b, 0, j)),
            pl.BlockSpec((None, HEADS, tnew), lambda b, j: (b, 0, 0)),
        ],
        out_specs=new_rows,
        out_shape=jax.ShapeDtypeStruct((t, d), BF16),
        scratch_shapes=[pltpu.VMEM((tnew, HEADS), F32), pltpu.VMEM((tnew, HEADS), F32), pltpu.VMEM((tnew, d), F32)],
        compiler_params=_params(("parallel", "arbitrary")),
        name="attn_sample",
    )(q, k_cache, v_cache, k_new, v_new, c_new, ct_cache, ct_new)


def _oproj_kernel(x_ref, o_ref, wo_ref, g_ref, b_ref, out_ref):
    y = DN_ALPHA * x_ref[...] + jnp.dot(o_ref[...], wo_ref[...], preferred_element_type=F32)
    out_ref[...] = _layer_norm(y, g_ref[...], b_ref[...])


def _oproj_ln(x, o, wo, g, b, *, tm=512):
    t, d = x.shape
    row = pl.BlockSpec((tm, d), lambda i: (i, 0))
    vec = pl.BlockSpec((1, d), lambda i: (0, 0))
    return pl.pallas_call(
        _oproj_kernel,
        grid=(t // tm,),
        in_specs=[row, row, pl.BlockSpec((d, d), lambda i: (0, 0)), vec, vec],
        out_specs=row,
        out_shape=jax.ShapeDtypeStruct((t, d), F32),
        compiler_params=_params(("parallel",)),
        name="oproj_ln",
    )(x, o, wo, g, b)


def kernel(x_prompt, x_sample, state_pool, cache_fox_k, cache_fox_v, cache_fox_logf, ln_g, ln_b,
           ffn_w1, ffn_w3, ffn_w2, pool_w, pool_scale, fox_w_in, fox_b_f, fox_w_o):
    d = D_MODEL
    _, seq, _ = x_prompt.shape
    nb, tnew, _ = x_sample.shape
    past = cache_fox_k.shape[2]
    ns = nb * tnew

    def ffn(x, i, s, ln_idx):
        return _ffn_ln(x, ffn_w1[i, s].astype(BF16), ffn_w3[i, s].astype(BF16), ffn_w2[i, s].astype(BF16),
                       ln_g[i, ln_idx][None], ln_b[i, ln_idx][None])

    x = jnp.concatenate([x_prompt.reshape(seq, d), x_sample.reshape(ns, d)], axis=0)

    x = ffn(x, 0, 0, 0)
    xp = x[:seq].reshape(1, seq, d)
    xs = x[seq:].reshape(nb, tnew, d)
    pool_prompt = xp[:, seq - POOL_STATE:][None]
    pool_sample = jnp.concatenate([state_pool[0], xs], axis=1)[:, -POOL_STATE:][None]
    pw = pool_w[0].astype(BF16)
    ps, g1, b1 = pool_scale[0][None], ln_g[0, 1][None], ln_b[0, 1][None]
    tm = 512
    mp = _pool_ln(xp, xp, lambda bi, i: (bi, jnp.maximum(i * (tm // HALO) - 1, 0), 0), pw, ps, g1, b1,
                  bb=1, tm=tm, start_pos=0, zero_first=True)
    prev = jnp.pad(state_pool[0], ((0, 0), (HALO - POOL_STATE, 0), (0, 0)))
    ms = _pool_ln(xs, prev, lambda bi, i: (bi, 0, 0), pw, ps, g1, b1,
                  bb=nb, tm=tnew, start_pos=past, zero_first=False)
    x = jnp.concatenate([mp.reshape(seq, d), ms.reshape(ns, d)], axis=0)
    x = ffn(x, 0, 1, 2)

    x = ffn(x, 1, 0, 0)
    w_in = fox_w_in[0]
    wq, wk, wv = (w_in[:, n * d:(n + 1) * d].astype(BF16) for n in range(3))
    wf = jnp.pad(w_in[:, 3 * d:], ((0, 0), (0, LANES - HEADS))).astype(BF16)
    bf = jnp.pad(fox_b_f[0], (0, LANES - HEADS))[None]
    q, k, v, kb, vb, lf = _fox_proj(x, wq, wk, wv, wf, bf)

    c_p = _cumsum_time(lf[:seq][None], tc=512)[0]
    o_p = _attn_prompt(q[:seq], kb[:seq], vb[:seq], c_p, c_p[:, :HEADS].T)

    lf_cache = jnp.pad(cache_fox_logf[0], ((0, 0), (0, 0), (0, LANES - HEADS)))
    lf_all = jnp.concatenate([lf_cache, lf[seq:].reshape(nb, tnew, LANES)], axis=1)
    c_s = _cumsum_time(lf_all, tc=(past + tnew) // 3)
    ct_s = c_s[:, :, :HEADS].transpose(0, 2, 1)
    o_s = _attn_sample(q[seq:], cache_fox_k[0].reshape(nb, past, d), cache_fox_v[0].reshape(nb, past, d),
                       k[seq:], v[seq:], c_s[:, past:].reshape(ns, LANES), ct_s[:, :, :past], ct_s[:, :, past:],
                       tnew=tnew)

    o = jnp.concatenate([o_p, o_s], axis=0)
    x = _oproj_ln(x, o, fox_w_o[0].astype(BF16), ln_g[1, 1][None], ln_b[1, 1][None])
    x = ffn(x, 1, 1, 2)

    shp = (HEADS, HEAD_DIM)
    return (x[:seq].reshape(1, seq, d), x[seq:].reshape(nb, tnew, d), pool_prompt, pool_sample,
            k[:seq].reshape(1, 1, seq, *shp), v[:seq].reshape(1, 1, seq, *shp), lf[:seq, :HEADS].reshape(1, 1, seq, HEADS),
            k[seq:].reshape(1, nb, tnew, *shp), v[seq:].reshape(1, nb, tnew, *shp),
            lf[seq:, :HEADS].reshape(1, nb, tnew, HEADS))
```

```python
import functools
import math

import jax
import jax.numpy as jnp
from jax import lax
from jax.experimental import pallas as pl
from jax.experimental.pallas import tpu as pltpu

F32 = jnp.float32
BF16 = jnp.bfloat16

D_MODEL = 2048
DEPTH = 2
POOL_WINDOWS = (2, 4, 8, 16)
POOL_GROUP = D_MODEL // len(POOL_WINDOWS)
POOL_STATE = max(POOL_WINDOWS) - 1
HALO = POOL_STATE + 1
HEAD_DIM = 128
HEADS = D_MODEL // HEAD_DIM
LN_EPS = 1e-5
DN_ALPHA = (2 * DEPTH) ** 0.25
LOG2E = math.log2(math.e)
Q_SCALE = HEAD_DIM ** -0.5 * LOG2E
NEG_BIG = -1e30
SKIP_LOG2 = 150.0
LANES = 128
SUBLANES = 8
VMEM_LIMIT = 60 * 1024 * 1024


def _params(semantics):
    return pltpu.CompilerParams(dimension_semantics=semantics, vmem_limit_bytes=VMEM_LIMIT)


def _layer_norm(y, g, b):
    mu = jnp.mean(y, axis=-1, keepdims=True)
    yc = y - mu
    var = jnp.mean(yc * yc, axis=-1, keepdims=True)
    return yc * lax.rsqrt(var + LN_EPS) * g + b


def _ffn_kernel(x_ref, w1_ref, w3_ref, w2_ref, g_ref, b_ref, o_ref, xb_ref):
    j = pl.program_id(1)

    @pl.when(j == 0)
    def _():
        xb_ref[...] = x_ref[...].astype(BF16)
        o_ref[...] = jnp.zeros_like(o_ref)

    xb = xb_ref[...]
    h1 = jnp.dot(xb, w1_ref[...], preferred_element_type=F32)
    h3 = jnp.dot(xb, w3_ref[...], preferred_element_type=F32)
    gate = (h1 * jax.nn.sigmoid(h1) * h3).astype(BF16)
    o_ref[...] += jnp.dot(gate, w2_ref[...], preferred_element_type=F32)

    @pl.when(j == pl.num_programs(1) - 1)
    def _():
        y = DN_ALPHA * x_ref[...] + 0.5 * o_ref[...]
        o_ref[...] = _layer_norm(y, g_ref[...], b_ref[...])


def _ffn_ln(x, w1, w3, w2, g, b, *, tm, tf=512):
    t, d = x.shape
    f = w1.shape[1]
    return pl.pallas_call(
        _ffn_kernel,
        grid=(t // tm, f // tf),
        in_specs=[
            pl.BlockSpec((tm, d), lambda i, j: (i, 0), pipeline_mode=pl.Buffered(1)),
            pl.BlockSpec((d, tf), lambda i, j: (0, j)),
            pl.BlockSpec((d, tf), lambda i, j: (0, j)),
            pl.BlockSpec((tf, d), lambda i, j: (j, 0)),
            pl.BlockSpec((1, d), lambda i, j: (0, 0)),
            pl.BlockSpec((1, d), lambda i, j: (0, 0)),
        ],
        out_specs=pl.BlockSpec((tm, d), lambda i, j: (i, 0)),
        out_shape=jax.ShapeDtypeStruct((t, d), F32),
        scratch_shapes=[pltpu.VMEM((tm, d), BF16)],
        compiler_params=_params(("parallel", "arbitrary")),
        name="ffn_ln",
    )(x, w1, w3, w2, g, b)


def _pool_kernel(x_ref, halo_ref, pw_ref, ps_ref, g_ref, b_ref, o_ref, buf_ref, *, tm, start_pos, zero_first):
    i = pl.program_id(1)
    bb = x_ref.shape[0]
    x = x_ref[...]
    halo = halo_ref[...]
    if zero_first:
        halo = jnp.where(i == 0, 0.0, halo)
    pos = start_pos + i * tm + lax.broadcasted_iota(jnp.int32, (1, tm, 1), 1)
    ys = []
    for g, w in enumerate(POOL_WINDOWS):
        cols = slice(g * POOL_GROUP, (g + 1) * POOL_GROUP)
        xg = x[:, :, cols]
        buf_ref[:, 0:HALO, :] = halo[:, :, cols]
        buf_ref[:, HALO:, :] = xg
        win = xg
        for k in range(1, w):
            win = win + buf_ref[:, HALO - k:HALO - k + tm, :]
        cnt = jnp.minimum(pos + 1, w).astype(F32)
        diff = (win / cnt - xg).reshape(bb * tm, POOL_GROUP)
        ys.append(jnp.dot(diff.astype(BF16), pw_ref[g], preferred_element_type=F32))
    y = jnp.concatenate(ys, axis=-1) * ps_ref[...]
    out = _layer_norm(DN_ALPHA * x.reshape(bb * tm, D_MODEL) + y, g_ref[...], b_ref[...])
    o_ref[...] = out.reshape(bb, tm, D_MODEL)


def _pool_ln(x, halo_src, halo_map, pw, ps, g, b, *, bb, tm, start_pos, zero_first):
    nb, t, d = x.shape
    kern = functools.partial(_pool_kernel, tm=tm, start_pos=start_pos, zero_first=zero_first)
    return pl.pallas_call(
        kern,
        grid=(nb // bb, t // tm),
        in_specs=[
            pl.BlockSpec((bb, tm, d), lambda bi, i: (bi, i, 0)),
            pl.BlockSpec((bb, HALO, d), halo_map),
            pl.BlockSpec((len(POOL_WINDOWS), POOL_GROUP, POOL_GROUP), lambda bi, i: (0, 0, 0)),
            pl.BlockSpec((1, d), lambda bi, i: (0, 0)),
            pl.BlockSpec((1, d), lambda bi, i: (0, 0)),
            pl.BlockSpec((1, d), lambda bi, i: (0, 0)),
        ],
        out_specs=pl.BlockSpec((bb, tm, d), lambda bi, i: (bi, i, 0)),
        out_shape=jax.ShapeDtypeStruct((nb, t, d), F32),
        scratch_shapes=[pltpu.VMEM((bb, HALO + tm, POOL_GROUP), F32)],
        compiler_params=_params(("parallel", "arbitrary")),
        name="pool_ln",
    )(x, halo_src, pw, ps, g, b)


def _max_sq_norm_per_head(xb, first_head, acc):
    sq = xb.astype(F32)
    sq = sq * sq
    lane = lax.broadcasted_iota(jnp.int32, acc.shape, 1)
    for hh in range(xb.shape[1] // HEAD_DIM):
        row = jnp.sum(sq[:, hh * HEAD_DIM:(hh + 1) * HEAD_DIM], axis=-1, keepdims=True)
        acc = jnp.where(lane == first_head + hh, jnp.max(row, axis=0, keepdims=True), acc)
    return acc


def _proj_kernel(x_ref, wq_ref, wk_ref, wv_ref, wf_ref, bf_ref,
                 q_ref, k_ref, v_ref, kb_ref, vb_ref, lf_ref, qn_ref, kn_ref, xb_ref):
    j = pl.program_id(1)
    heads_per_step = q_ref.shape[1] // HEAD_DIM

    @pl.when(j == 0)
    def _():
        xb = x_ref[...].astype(BF16)
        xb_ref[...] = xb
        fl = jnp.dot(xb, wf_ref[...], preferred_element_type=F32) + bf_ref[...]
        lf_ref[...] = jnp.minimum(fl, 0.0) - jnp.log1p(jnp.exp(-jnp.abs(fl)))
        qn_ref[...] = jnp.zeros_like(qn_ref)
        kn_ref[...] = jnp.zeros_like(kn_ref)

    xb = xb_ref[...]
    qb = (jnp.dot(xb, wq_ref[...], preferred_element_type=F32) * Q_SCALE).astype(BF16)
    q_ref[...] = qb
    k = jnp.dot(xb, wk_ref[...], preferred_element_type=F32)
    kb = k.astype(BF16)
    k_ref[...] = k
    kb_ref[...] = kb
    v = jnp.dot(xb, wv_ref[...], preferred_element_type=F32)
    v_ref[...] = v
    vb_ref[...] = v.astype(BF16)
    qn_ref[...] = _max_sq_norm_per_head(qb, j * heads_per_step, qn_ref[...])
    kn_ref[...] = _max_sq_norm_per_head(kb, j * heads_per_step, kn_ref[...])


def _fox_proj(x, wq, wk, wv, wf, bf, *, tm=512, tn=512):
    t, d = x.shape
    row = pl.BlockSpec((tm, tn), lambda i, j: (i, j))
    wspec = pl.BlockSpec((d, tn), lambda i, j: (0, j))
    tile_stat = pl.BlockSpec((None, SUBLANES, LANES), lambda i, j: (i, 0, 0))
    return pl.pallas_call(
        _proj_kernel,
        grid=(t // tm, d // tn),
        in_specs=[
            pl.BlockSpec((tm, d), lambda i, j: (i, 0)),
            wspec, wspec, wspec,
            pl.BlockSpec((d, LANES), lambda i, j: (0, 0)),
            pl.BlockSpec((1, LANES), lambda i, j: (0, 0)),
        ],
        out_specs=[row, row, row, row, row, pl.BlockSpec((tm, LANES), lambda i, j: (i, 0)), tile_stat, tile_stat],
        out_shape=[
            jax.ShapeDtypeStruct((t, d), BF16),
            jax.ShapeDtypeStruct((t, d), F32),
            jax.ShapeDtypeStruct((t, d), F32),
            jax.ShapeDtypeStruct((t, d), BF16),
            jax.ShapeDtypeStruct((t, d), BF16),
            jax.ShapeDtypeStruct((t, LANES), F32),
            jax.ShapeDtypeStruct((t // tm, SUBLANES, LANES), F32),
            jax.ShapeDtypeStruct((t // tm, SUBLANES, LANES), F32),
        ],
        scratch_shapes=[pltpu.VMEM((tm, d), BF16)],
        compiler_params=_params(("parallel", "arbitrary")),
        name="fox_proj",
    )(x, wq, wk, wv, wf, bf)


def _cumsum_kernel(x_ref, o_ref, carry_ref, *, tc):
    @pl.when(pl.program_id(1) == 0)
    def _():
        carry_ref[...] = jnp.zeros_like(carry_ref)

    x = x_ref[...]
    r = lax.broadcasted_iota(jnp.int32, (tc, tc), 0)
    c = lax.broadcasted_iota(jnp.int32, (tc, tc), 1)
    tri = (c <= r).astype(BF16)
    hi = x.astype(BF16)
    rem = x - hi.astype(F32)
    mid = rem.astype(BF16)
    lo = (rem - mid.astype(F32)).astype(BF16)
    cs = (jnp.dot(tri, hi, preferred_element_type=F32)
          + jnp.dot(tri, mid, preferred_element_type=F32)
          + jnp.dot(tri, lo, preferred_element_type=F32)) + carry_ref[...]
    o_ref[...] = cs * LOG2E
    carry_ref[...] = cs[tc - 1:tc, :]


def _cumsum_time(x, *, tc):
    nb, t, w = x.shape
    return pl.pallas_call(
        functools.partial(_cumsum_kernel, tc=tc),
        grid=(nb, t // tc),
        in_specs=[pl.BlockSpec((None, tc, w), lambda b, j: (b, j, 0))],
        out_specs=pl.BlockSpec((None, tc, w), lambda b, j: (b, j, 0)),
        out_shape=jax.ShapeDtypeStruct((nb, t, w), F32),
        scratch_shapes=[pltpu.VMEM((1, w), F32)],
        compiler_params=_params(("parallel", "arbitrary")),
        name="cumsum_time",
    )(x)


def _qk(q_h, k_h):
    return lax.dot_general(q_h, k_h, (((1,), (1,)), ((), ())), preferred_element_type=F32)


def _attn_schedule(qmax, kmax, c2, tq, nsteps):
    nq = qmax.shape[0]
    qn, kn = jnp.sqrt(qmax), jnp.sqrt(kmax)
    c_first, c_last = c2[0::tq, :HEADS], c2[tq - 1::tq, :HEADS]
    bound = qn[:, None, :] * (kn[None, :, :] + kn[:, None, :]) + c_first[:, None, :] - c_last[None, :, :]
    tile = jnp.arange(nq, dtype=jnp.int32)
    skip = jnp.all(bound < -SKIP_LOG2, axis=-1) & (tile[None, :] < tile[:, None])
    jstart = jnp.argmin(skip.astype(jnp.int32), axis=1).astype(jnp.int32)
    cnt = tile - jstart + 1
    ends = jnp.cumsum(cnt)
    starts = ends - cnt
    n = jnp.arange(nsteps, dtype=jnp.int32)
    valid = n < ends[-1]
    qi = jnp.minimum(jnp.searchsorted(ends, n, side="right").astype(jnp.int32), nq - 1)
    kj = jstart[qi] + n - starts[qi]
    qi = jnp.where(valid, qi, nq - 1)
    kj = jnp.where(valid, kj, nq - 1)
    first = valid & (kj == jstart[qi])
    return qi, kj, valid.astype(jnp.int32) + 2 * first.astype(jnp.int32)


def _attn_prompt_kernel(qi_ref, kj_ref, fl_ref, q_ref, k_ref, v_ref, c_ref, ct_ref, o_ref,
                        m_ref, l_ref, acc_ref, crep_ref, *, tq):
    n = pl.program_id(0)
    flags = fl_ref[n]
    nchunk = tq // LANES

    @pl.when(flags >= 2)
    def _():
        m_ref[...] = jnp.full_like(m_ref, NEG_BIG)
        l_ref[...] = jnp.zeros_like(l_ref)
        acc_ref[...] = jnp.zeros_like(acc_ref)
        for h in range(HEADS):
            crep_ref[h] = jnp.broadcast_to(c_ref[:, h:h + 1], (tq, LANES))

    def sweep(masked):
        if masked:
            row = lax.broadcasted_iota(jnp.int32, (tq, LANES), 0)
            lane = lax.broadcasted_iota(jnp.int32, (tq, LANES), 1)
        for h in range(HEADS):
            sl = slice(h * HEAD_DIM, (h + 1) * HEAD_DIM)
            s = _qk(q_ref[:, sl], k_ref[:, sl])
            crep = crep_ref[h]
            chunks = []
            for c in range(nchunk):
                cs = slice(c * LANES, (c + 1) * LANES)
                sc = s[:, cs] + crep - ct_ref[h:h + 1, cs]
                if masked:
                    sc = jnp.where(lane + c * LANES <= row, sc, NEG_BIG)
                chunks.append(sc)
            mx = chunks[0]
            for sc in chunks[1:]:
                mx = jnp.maximum(mx, sc)
            m_prev = m_ref[h]
            m_new = jnp.maximum(m_prev, jnp.max(mx, axis=-1, keepdims=True))
            a = jnp.exp2(m_prev - m_new)
            ps = [jnp.exp2(sc - m_new) for sc in chunks]
            lsum = ps[0]
            for p in ps[1:]:
                lsum = lsum + p
            l_ref[h] = a * l_ref[h] + lsum
            p = jnp.concatenate(ps, axis=-1).astype(BF16)
            acc_ref[:, sl] = a * acc_ref[:, sl] + jnp.dot(p, v_ref[:, sl], preferred_element_type=F32)
            m_ref[h] = m_new

    diagonal = kj_ref[n] == qi_ref[n]

    @pl.when((flags >= 1) & jnp.logical_not(diagonal))
    def _():
        sweep(False)

    @pl.when((flags >= 1) & diagonal)
    def _():
        sweep(True)
        for h in range(HEADS):
            sl = slice(h * HEAD_DIM, (h + 1) * HEAD_DIM)
            l_tot = jnp.sum(l_ref[h], axis=-1, keepdims=True)
            o_ref[:, sl] = (acc_ref[:, sl] / l_tot).astype(BF16)


def _attn_prompt(q, kb, vb, c2, ct2, qmax, kmax, *, tq=512):
    s, d = q.shape
    nq = s // tq
    nsteps = nq * (nq + 1) // 2
    qi, kj, flags = _attn_schedule(qmax, kmax, c2, tq, nsteps)
    qrow = lambda n, qi, kj, fl: (qi[n], 0)
    krow = lambda n, qi, kj, fl: (kj[n], 0)
    stat = pltpu.VMEM((HEADS, tq, LANES), F32)
    return pl.pallas_call(
        functools.partial(_attn_prompt_kernel, tq=tq),
        grid_spec=pltpu.PrefetchScalarGridSpec(
            num_scalar_prefetch=3,
            grid=(nsteps,),
            in_specs=[
                pl.BlockSpec((tq, d), qrow),
                pl.BlockSpec((tq, d), krow),
                pl.BlockSpec((tq, d), krow),
                pl.BlockSpec((tq, LANES), qrow),
                pl.BlockSpec((HEADS, tq), lambda n, qi, kj, fl: (0, kj[n])),
            ],
            out_specs=pl.BlockSpec((tq, d), qrow),
            scratch_shapes=[stat, stat, pltpu.VMEM((tq, d), F32), stat],
        ),
        out_shape=jax.ShapeDtypeStruct((s, d), BF16),
        compiler_params=_params(("arbitrary",)),
        name="attn_prompt",
    )(qi, kj, flags, q, kb, vb, c2, ct2)


def _softmax_step(s, h, v_h, m_ref, l_ref, acc_ref):
    hs = slice(h, h + 1)
    sl = slice(h * HEAD_DIM, (h + 1) * HEAD_DIM)
    m_prev = m_ref[:, hs]
    m_new = jnp.maximum(m_prev, jnp.max(s, axis=-1, keepdims=True))
    a = jnp.exp2(m_prev - m_new)
    p = jnp.exp2(s - m_new)
    l_ref[:, hs] = a * l_ref[:, hs] + jnp.sum(p, axis=-1, keepdims=True)
    acc_ref[:, sl] = a * acc_ref[:, sl] + jnp.dot(p.astype(BF16), v_h, preferred_element_type=F32)
    m_ref[:, hs] = m_new


def _attn_sample_kernel(q_ref, kc_ref, vc_ref, kn_ref, vn_ref, cq_ref, ctc_ref, ctn_ref, o_ref,
                        m_ref, l_ref, acc_ref, *, tp):
    j = pl.program_id(1)
    tnew = q_ref.shape[0]

    @pl.when(j == 0)
    def _():
        m_ref[...] = jnp.full_like(m_ref, NEG_BIG)
        l_ref[...] = jnp.zeros_like(l_ref)
        acc_ref[...] = jnp.zeros_like(acc_ref)

    for h in range(HEADS):
        sl = slice(h * HEAD_DIM, (h + 1) * HEAD_DIM)
        k_h = kc_ref[pl.ds(h, tp, stride=HEADS), :].astype(BF16)
        v_h = vc_ref[pl.ds(h, tp, stride=HEADS), :].astype(BF16)
        s = _qk(q_ref[:, sl], k_h) + cq_ref[:, h:h + 1] - ctc_ref[h:h + 1, :]
        _softmax_step(s, h, v_h, m_ref, l_ref, acc_ref)

    @pl.when(j == pl.num_programs(1) - 1)
    def _():
        rows = lax.broadcasted_iota(jnp.int32, (tnew, tnew), 0)
        cols = lax.broadcasted_iota(jnp.int32, (tnew, tnew), 1)
        visible = cols <= rows
        for h in range(HEADS):
            sl = slice(h * HEAD_DIM, (h + 1) * HEAD_DIM)
            s = _qk(q_ref[:, sl], kn_ref[:, sl]) + cq_ref[:, h:h + 1] - ctn_ref[h:h + 1, :]
            s = jnp.where(visible, s, NEG_BIG)
            _softmax_step(s, h, vn_ref[:, sl], m_ref, l_ref, acc_ref)
            o_ref[:, sl] = (acc_ref[:, sl] / l_ref[:, h:h + 1]).astype(BF16)


def _attn_sample(q, k_cache, v_cache, kb_new, vb_new, c_new, ct_cache, ct_new, *, tnew, tp=512):
    t, d = q.shape
    nb, past_rows, _ = k_cache.shape
    new_rows = pl.BlockSpec((tnew, d), lambda b, j: (b, 0))
    cache = pl.BlockSpec((None, tp * HEADS, HEAD_DIM), lambda b, j: (b, j, 0))
    return pl.pallas_call(
        functools.partial(_attn_sample_kernel, tp=tp),
        grid=(nb, past_rows // (tp * HEADS)),
        in_specs=[
            new_rows, cache, cache, new_rows, new_rows,
            pl.BlockSpec((tnew, LANES), lambda b, j: (b, 0)),
            pl.BlockSpec((None, HEADS, tp), lambda b, j: (b, 0, j)),
            pl.BlockSpec((None, HEADS, tnew), lambda b, j: (b, 0, 0)),
        ],
        out_specs=new_rows,
        out_shape=jax.ShapeDtypeStruct((t, d), BF16),
        scratch_shapes=[pltpu.VMEM((tnew, HEADS), F32), pltpu.VMEM((tnew, HEADS), F32), pltpu.VMEM((tnew, d), F32)],
        compiler_params=_params(("parallel", "arbitrary")),
        name="attn_sample",
    )(q, k_cache, v_cache, kb_new, vb_new, c_new, ct_cache, ct_new)


def _oproj_kernel(x_ref, o_ref, wo_ref, g_ref, b_ref, out_ref):
    y = DN_ALPHA * x_ref[...] + jnp.dot(o_ref[...], wo_ref[...], preferred_element_type=F32)
    out_ref[...] = _layer_norm(y, g_ref[...], b_ref[...])


def _oproj_ln(x, o, wo, g, b, *, tm=512):
    t, d = x.shape
    row = pl.BlockSpec((tm, d), lambda i: (i, 0))
    vec = pl.BlockSpec((1, d), lambda i: (0, 0))
    return pl.pallas_call(
        _oproj_kernel,
        grid=(t // tm,),
        in_specs=[row, row, pl.BlockSpec((d, d), lambda i: (0, 0)), vec, vec],
        out_specs=row,
        out_shape=jax.ShapeDtypeStruct((t, d), F32),
        compiler_params=_params(("parallel",)),
        name="oproj_ln",
    )(x, o, wo, g, b)


def kernel(x_prompt, x_sample, state_pool, cache_fox_k, cache_fox_v, cache_fox_logf, ln_g, ln_b,
           ffn_w1, ffn_w3, ffn_w2, pool_w, pool_scale, fox_w_in, fox_b_f, fox_w_o):
    d = D_MODEL
    _, seq, _ = x_prompt.shape
    nb, tnew, _ = x_sample.shape
    past = cache_fox_k.shape[2]
    ns = nb * tnew
    w1b, w3b, w2b = ffn_w1.astype(BF16), ffn_w3.astype(BF16), ffn_w2.astype(BF16)

    def ffn(xp, xs, i, s, ln_idx):
        args = (w1b[i, s], w3b[i, s], w2b[i, s], ln_g[i, ln_idx][None], ln_b[i, ln_idx][None])
        return _ffn_ln(xp, *args, tm=1024), _ffn_ln(xs, *args, tm=ns)

    xp, xs = x_prompt.reshape(seq, d), x_sample.reshape(ns, d)

    xp, xs = ffn(xp, xs, 0, 0, 0)
    xp3, xs3 = xp.reshape(1, seq, d), xs.reshape(nb, tnew, d)
    pool_prompt = xp3[:, seq - POOL_STATE:][None]
    pool_sample = jnp.concatenate([state_pool[0], xs3], axis=1)[:, -POOL_STATE:][None]
    pw = pool_w[0].astype(BF16)
    ps, g1, b1 = pool_scale[0][None], ln_g[0, 1][None], ln_b[0, 1][None]
    tm = 512
    xp = _pool_ln(xp3, xp3, lambda bi, i: (bi, jnp.maximum(i * (tm // HALO) - 1, 0), 0), pw, ps, g1, b1,
                  bb=1, tm=tm, start_pos=0, zero_first=True).reshape(seq, d)
    prev = jnp.pad(state_pool[0], ((0, 0), (HALO - POOL_STATE, 0), (0, 0)))
    xs = _pool_ln(xs3, prev, lambda bi, i: (bi, 0, 0), pw, ps, g1, b1,
                  bb=nb, tm=tnew, start_pos=past, zero_first=False).reshape(ns, d)
    xp, xs = ffn(xp, xs, 0, 1, 2)

    xp, xs = ffn(xp, xs, 1, 0, 0)
    w_in = fox_w_in[0]
    wq, wk, wv = (w_in[:, n * d:(n + 1) * d].astype(BF16) for n in range(3))
    wf = jnp.pad(w_in[:, 3 * d:], ((0, 0), (0, LANES - HEADS))).astype(BF16)
    bf = jnp.pad(fox_b_f[0], (0, LANES - HEADS))[None]
    q_p, k_p, v_p, kb_p, vb_p, lf_p, qmax, kmax = _fox_proj(xp, wq, wk, wv, wf, bf)
    q_s, k_s, v_s, kb_s, vb_s, lf_s, _, _ = _fox_proj(xs, wq, wk, wv, wf, bf)

    c_p = _cumsum_time(lf_p[None], tc=512)[0]
    o_p = _attn_prompt(q_p, kb_p, vb_p, c_p, c_p[:, :HEADS].T, qmax[:, 0, :HEADS], kmax[:, 0, :HEADS])

    lf_cache = jnp.pad(cache_fox_logf[0], ((0, 0), (0, 0), (0, LANES - HEADS)))
    lf_all = jnp.concatenate([lf_cache, lf_s.reshape(nb, tnew, LANES)], axis=1)
    c_s = _cumsum_time(lf_all, tc=(past + tnew) // 3)
    ct_s = c_s[:, :, :HEADS].transpose(0, 2, 1)
    o_s = _attn_sample(q_s, cache_fox_k[0].reshape(nb, past * HEADS, HEAD_DIM),
                       cache_fox_v[0].reshape(nb, past * HEADS, HEAD_DIM), kb_s, vb_s,
                       c_s[:, past:].reshape(ns, LANES), ct_s[:, :, :past], ct_s[:, :, past:], tnew=tnew)

    wo = fox_w_o[0].astype(BF16)
    g1, b1 = ln_g[1, 1][None], ln_b[1, 1][None]
    xp, xs = _oproj_ln(xp, o_p, wo, g1, b1), _oproj_ln(xs, o_s, wo, g1, b1)
    xp, xs = ffn(xp, xs, 1, 1, 2)

    shp = (HEADS, HEAD_DIM)
    return (xp.reshape(1, seq, d), xs.reshape(nb, tnew, d), pool_prompt, pool_sample,
            k_p.reshape(1, 1, seq, *shp), v_p.reshape(1, 1, seq, *shp), lf_p[:, :HEADS].reshape(1, 1, seq, HEADS),
            k_s.reshape(1, nb, tnew, *shp), v_s.reshape(1, nb, tnew, *shp),
            lf_s[:, :HEADS].reshape(1, nb, tnew, HEADS))
```

```python
import functools
import math

import jax
import jax.numpy as jnp
from jax import lax
from jax.experimental import pallas as pl
from jax.experimental.pallas import tpu as pltpu

F32 = jnp.float32
BF16 = jnp.bfloat16

D_MODEL = 2048
DEPTH = 2
POOL_WINDOWS = (2, 4, 8, 16)
POOL_GROUP = D_MODEL // len(POOL_WINDOWS)
POOL_STATE = max(POOL_WINDOWS) - 1
HALO = POOL_STATE + 1
HEAD_DIM = 128
HEADS = D_MODEL // HEAD_DIM
LN_EPS = 1e-5
DN_ALPHA = (2 * DEPTH) ** 0.25
LOG2E = math.log2(math.e)
Q_SCALE = HEAD_DIM ** -0.5 * LOG2E
NEG_BIG = -1e30
SKIP_LOG2 = 150.0
LANES = 128
SUBLANES = 8
VMEM_LIMIT = 60 * 1024 * 1024


def _params(semantics):
    return pltpu.CompilerParams(dimension_semantics=semantics, vmem_limit_bytes=VMEM_LIMIT)


def _layer_norm(y, g, b):
    mu = jnp.mean(y, axis=-1, keepdims=True)
    yc = y - mu
    var = jnp.mean(yc * yc, axis=-1, keepdims=True)
    return yc * lax.rsqrt(var + LN_EPS) * g + b


def _ffn_kernel(x_ref, w1_ref, w3_ref, w2_ref, g_ref, b_ref, o_ref, xb_ref):
    j = pl.program_id(1)

    @pl.when(j == 0)
    def _():
        xb_ref[...] = x_ref[...].astype(BF16)
        o_ref[...] = jnp.zeros_like(o_ref)

    xb = xb_ref[...]
    h1 = jnp.dot(xb, w1_ref[...], preferred_element_type=F32)
    h3 = jnp.dot(xb, w3_ref[...], preferred_element_type=F32)
    gate = (h1 * jax.nn.sigmoid(h1) * h3).astype(BF16)
    o_ref[...] += jnp.dot(gate, w2_ref[...], preferred_element_type=F32)

    @pl.when(j == pl.num_programs(1) - 1)
    def _():
        y = DN_ALPHA * x_ref[...] + 0.5 * o_ref[...]
        o_ref[...] = _layer_norm(y, g_ref[...], b_ref[...])


def _ffn_ln(x, w1, w3, w2, g, b, *, tm, tf=512):
    t, d = x.shape
    f = w1.shape[1]
    return pl.pallas_call(
        _ffn_kernel,
        grid=(t // tm, f // tf),
        in_specs=[
            pl.BlockSpec((tm, d), lambda i, j: (i, 0), pipeline_mode=pl.Buffered(1)),
            pl.BlockSpec((d, tf), lambda i, j: (0, j)),
            pl.BlockSpec((d, tf), lambda i, j: (0, j)),
            pl.BlockSpec((tf, d), lambda i, j: (j, 0)),
            pl.BlockSpec((1, d), lambda i, j: (0, 0)),
            pl.BlockSpec((1, d), lambda i, j: (0, 0)),
        ],
        out_specs=pl.BlockSpec((tm, d), lambda i, j: (i, 0)),
        out_shape=jax.ShapeDtypeStruct((t, d), F32),
        scratch_shapes=[pltpu.VMEM((tm, d), BF16)],
        compiler_params=_params(("parallel", "arbitrary")),
        name="ffn_ln",
    )(x, w1, w3, w2, g, b)


def _pool_kernel(x_ref, halo_ref, pw_ref, ps_ref, g_ref, b_ref, o_ref, buf_ref, *, tm, start_pos, zero_first):
    i = pl.program_id(1)
    bb = x_ref.shape[0]
    x = x_ref[...]
    halo = halo_ref[...]
    if zero_first:
        halo = jnp.where(i == 0, 0.0, halo)
    pos = start_pos + i * tm + lax.broadcasted_iota(jnp.int32, (1, tm, 1), 1)
    ys = []
    for g, w in enumerate(POOL_WINDOWS):
        cols = slice(g * POOL_GROUP, (g + 1) * POOL_GROUP)
        xg = x[:, :, cols]
        buf_ref[:, 0:HALO, :] = halo[:, :, cols]
        buf_ref[:, HALO:, :] = xg
        win = xg
        for k in range(1, w):
            win = win + buf_ref[:, HALO - k:HALO - k + tm, :]
        cnt = jnp.minimum(pos + 1, w).astype(F32)
        diff = (win / cnt - xg).reshape(bb * tm, POOL_GROUP)
        ys.append(jnp.dot(diff.astype(BF16), pw_ref[g], preferred_element_type=F32))
    y = jnp.concatenate(ys, axis=-1) * ps_ref[...]
    out = _layer_norm(DN_ALPHA * x.reshape(bb * tm, D_MODEL) + y, g_ref[...], b_ref[...])
    o_ref[...] = out.reshape(bb, tm, D_MODEL)


def _pool_ln(x, halo_src, halo_map, pw, ps, g, b, *, bb, tm, start_pos, zero_first):
    nb, t, d = x.shape
    kern = functools.partial(_pool_kernel, tm=tm, start_pos=start_pos, zero_first=zero_first)
    return pl.pallas_call(
        kern,
        grid=(nb // bb, t // tm),
        in_specs=[
            pl.BlockSpec((bb, tm, d), lambda bi, i: (bi, i, 0)),
            pl.BlockSpec((bb, HALO, d), halo_map),
            pl.BlockSpec((len(POOL_WINDOWS), POOL_GROUP, POOL_GROUP), lambda bi, i: (0, 0, 0)),
            pl.BlockSpec((1, d), lambda bi, i: (0, 0)),
            pl.BlockSpec((1, d), lambda bi, i: (0, 0)),
            pl.BlockSpec((1, d), lambda bi, i: (0, 0)),
        ],
        out_specs=pl.BlockSpec((bb, tm, d), lambda bi, i: (bi, i, 0)),
        out_shape=jax.ShapeDtypeStruct((nb, t, d), F32),
        scratch_shapes=[pltpu.VMEM((bb, HALO + tm, POOL_GROUP), F32)],
        compiler_params=_params(("parallel", "arbitrary")),
        name="pool_ln",
    )(x, halo_src, pw, ps, g, b)


def _max_sq_norm_per_head(xb, first_head, acc):
    sq = xb.astype(F32)
    sq = sq * sq
    lane = lax.broadcasted_iota(jnp.int32, acc.shape, 1)
    for hh in range(xb.shape[1] // HEAD_DIM):
        row = jnp.sum(sq[:, hh * HEAD_DIM:(hh + 1) * HEAD_DIM], axis=-1, keepdims=True)
        acc = jnp.where(lane == first_head + hh, jnp.max(row, axis=0, keepdims=True), acc)
    return acc


def _proj_kernel(x_ref, wq_ref, wk_ref, wv_ref, wf_ref, bf_ref,
                 q_ref, k_ref, v_ref, kb_ref, vb_ref, lf_ref, qn_ref, kn_ref, xb_ref):
    j = pl.program_id(1)
    heads_per_step = q_ref.shape[1] // HEAD_DIM

    @pl.when(j == 0)
    def _():
        xb = x_ref[...].astype(BF16)
        xb_ref[...] = xb
        fl = jnp.dot(xb, wf_ref[...], preferred_element_type=F32) + bf_ref[...]
        lf_ref[...] = jnp.minimum(fl, 0.0) - jnp.log1p(jnp.exp(-jnp.abs(fl)))
        qn_ref[...] = jnp.zeros_like(qn_ref)
        kn_ref[...] = jnp.zeros_like(kn_ref)

    xb = xb_ref[...]
    qb = (jnp.dot(xb, wq_ref[...], preferred_element_type=F32) * Q_SCALE).astype(BF16)
    q_ref[...] = qb
    k = jnp.dot(xb, wk_ref[...], preferred_element_type=F32)
    kb = k.astype(BF16)
    kb_ref[...] = kb
    v = jnp.dot(xb, wv_ref[...], preferred_element_type=F32)
    vb_ref[...] = v.astype(BF16)
    tm = x_ref.shape[0]
    for hh in range(heads_per_step):
        rows = pl.ds(j * heads_per_step + hh, tm, stride=HEADS)
        k_ref[rows, :] = k[:, hh * HEAD_DIM:(hh + 1) * HEAD_DIM]
        v_ref[rows, :] = v[:, hh * HEAD_DIM:(hh + 1) * HEAD_DIM]
    qn_ref[...] = _max_sq_norm_per_head(qb, j * heads_per_step, qn_ref[...])
    kn_ref[...] = _max_sq_norm_per_head(kb, j * heads_per_step, kn_ref[...])


def _fox_proj(x, wq, wk, wv, wf, bf, *, tm=512, tn=512):
    t, d = x.shape
    row = pl.BlockSpec((tm, tn), lambda i, j: (i, j))
    wspec = pl.BlockSpec((d, tn), lambda i, j: (0, j))
    tile_stat = pl.BlockSpec((None, SUBLANES, LANES), lambda i, j: (i, 0, 0))
    by_head = pl.BlockSpec((tm * HEADS, HEAD_DIM), lambda i, j: (i, 0))
    return pl.pallas_call(
        _proj_kernel,
        grid=(t // tm, d // tn),
        in_specs=[
            pl.BlockSpec((tm, d), lambda i, j: (i, 0)),
            wspec, wspec, wspec,
            pl.BlockSpec((d, LANES), lambda i, j: (0, 0)),
            pl.BlockSpec((1, LANES), lambda i, j: (0, 0)),
        ],
        out_specs=[row, by_head, by_head, row, row, pl.BlockSpec((tm, LANES), lambda i, j: (i, 0)), tile_stat, tile_stat],
        out_shape=[
            jax.ShapeDtypeStruct((t, d), BF16),
            jax.ShapeDtypeStruct((t * HEADS, HEAD_DIM), F32),
            jax.ShapeDtypeStruct((t * HEADS, HEAD_DIM), F32),
            jax.ShapeDtypeStruct((t, d), BF16),
            jax.ShapeDtypeStruct((t, d), BF16),
            jax.ShapeDtypeStruct((t, LANES), F32),
            jax.ShapeDtypeStruct((t // tm, SUBLANES, LANES), F32),
            jax.ShapeDtypeStruct((t // tm, SUBLANES, LANES), F32),
        ],
        scratch_shapes=[pltpu.VMEM((tm, d), BF16)],
        compiler_params=_params(("parallel", "arbitrary")),
        name="fox_proj",
    )(x, wq, wk, wv, wf, bf)


def _cumsum_kernel(x_ref, o_ref, carry_ref, *, tc):
    @pl.when(pl.program_id(1) == 0)
    def _():
        carry_ref[...] = jnp.zeros_like(carry_ref)

    x = x_ref[...]
    r = lax.broadcasted_iota(jnp.int32, (tc, tc), 0)
    c = lax.broadcasted_iota(jnp.int32, (tc, tc), 1)
    tri = (c <= r).astype(BF16)
    hi = x.astype(BF16)
    rem = x - hi.astype(F32)
    mid = rem.astype(BF16)
    lo = (rem - mid.astype(F32)).astype(BF16)
    cs = (jnp.dot(tri, hi, preferred_element_type=F32)
          + jnp.dot(tri, mid, preferred_element_type=F32)
          + jnp.dot(tri, lo, preferred_element_type=F32)) + carry_ref[...]
    o_ref[...] = cs * LOG2E
    carry_ref[...] = cs[tc - 1:tc, :]


def _cumsum_time(x, *, tc):
    nb, t, w = x.shape
    return pl.pallas_call(
        functools.partial(_cumsum_kernel, tc=tc),
        grid=(nb, t // tc),
        in_specs=[pl.BlockSpec((None, tc, w), lambda b, j: (b, j, 0))],
        out_specs=pl.BlockSpec((None, tc, w), lambda b, j: (b, j, 0)),
        out_shape=jax.ShapeDtypeStruct((nb, t, w), F32),
        scratch_shapes=[pltpu.VMEM((1, w), F32)],
        compiler_params=_params(("parallel", "arbitrary")),
        name="cumsum_time",
    )(x)


def _qk(q_h, k_h):
    return lax.dot_general(q_h, k_h, (((1,), (1,)), ((), ())), preferred_element_type=F32)


def _attn_schedule(qmax, kmax, c2, tq, nsteps):
    nq = qmax.shape[0]
    qn, kn = jnp.sqrt(qmax), jnp.sqrt(kmax)
    c_first, c_last = c2[0::tq, :HEADS], c2[tq - 1::tq, :HEADS]
    bound = qn[:, None, :] * (kn[None, :, :] + kn[:, None, :]) + c_first[:, None, :] - c_last[None, :, :]
    tile = jnp.arange(nq, dtype=jnp.int32)
    skip = jnp.all(bound < -SKIP_LOG2, axis=-1) & (tile[None, :] < tile[:, None])
    jstart = jnp.argmin(skip.astype(jnp.int32), axis=1).astype(jnp.int32)
    cnt = tile - jstart + 1
    ends = jnp.cumsum(cnt)
    starts = ends - cnt
    n = jnp.arange(nsteps, dtype=jnp.int32)
    valid = n < ends[-1]
    qi = jnp.minimum(jnp.sum((ends[None, :] <= n[:, None]).astype(jnp.int32), axis=1), nq - 1)
    kj = jstart[qi] + n - starts[qi]
    qi = jnp.where(valid, qi, nq - 1)
    kj = jnp.where(valid, kj, nq - 1)
    first = valid & (kj == jstart[qi])
    return qi, kj, valid.astype(jnp.int32) + 2 * first.astype(jnp.int32)


def _attn_prompt_kernel(qi_ref, kj_ref, fl_ref, q_ref, k_ref, v_ref, c_ref, ct_ref, o_ref,
                        m_ref, l_ref, acc_ref, crep_ref, *, tq):
    n = pl.program_id(0)
    flags = fl_ref[n]
    nchunk = tq // LANES

    @pl.when(flags >= 2)
    def _():
        m_ref[...] = jnp.full_like(m_ref, NEG_BIG)
        l_ref[...] = jnp.zeros_like(l_ref)
        acc_ref[...] = jnp.zeros_like(acc_ref)
        for h in range(HEADS):
            crep_ref[h] = jnp.broadcast_to(c_ref[:, h:h + 1], (tq, LANES))

    def sweep(masked):
        if masked:
            row = lax.broadcasted_iota(jnp.int32, (tq, LANES), 0)
            lane = lax.broadcasted_iota(jnp.int32, (tq, LANES), 1)
        for h in range(HEADS):
            sl = slice(h * HEAD_DIM, (h + 1) * HEAD_DIM)
            s = _qk(q_ref[:, sl], k_ref[:, sl])
            crep = crep_ref[h]
            chunks = []
            for c in range(nchunk):
                cs = slice(c * LANES, (c + 1) * LANES)
                sc = s[:, cs] + crep - ct_ref[h:h + 1, cs]
                if masked:
                    sc = jnp.where(lane + c * LANES <= row, sc, NEG_BIG)
                chunks.append(sc)
            mx = chunks[0]
            for sc in chunks[1:]:
                mx = jnp.maximum(mx, sc)
            m_prev = m_ref[h]
            m_new = jnp.maximum(m_prev, jnp.max(mx, axis=-1, keepdims=True))
            a = jnp.exp2(m_prev - m_new)
            ps = [jnp.exp2(sc - m_new) for sc in chunks]
            lsum = ps[0]
            for p in ps[1:]:
                lsum = lsum + p
            l_ref[h] = a * l_ref[h] + lsum
            p = jnp.concatenate(ps, axis=-1).astype(BF16)
            acc_ref[:, sl] = a * acc_ref[:, sl] + jnp.dot(p, v_ref[:, sl], preferred_element_type=F32)
            m_ref[h] = m_new

    diagonal = kj_ref[n] == qi_ref[n]

    @pl.when((flags >= 1) & jnp.logical_not(diagonal))
    def _():
        sweep(False)

    @pl.when((flags >= 1) & diagonal)
    def _():
        sweep(True)
        for h in range(HEADS):
            sl = slice(h * HEAD_DIM, (h + 1) * HEAD_DIM)
            l_tot = jnp.sum(l_ref[h], axis=-1, keepdims=True)
            o_ref[:, sl] = (acc_ref[:, sl] / l_tot).astype(BF16)


def _attn_prompt(q, kb, vb, c2, ct2, qmax, kmax, *, tq=512):
    s, d = q.shape
    nq = s // tq
    nsteps = nq * (nq + 1) // 2
    qi, kj, flags = _attn_schedule(qmax, kmax, c2, tq, nsteps)
    qrow = lambda n, qi, kj, fl: (qi[n], 0)
    krow = lambda n, qi, kj, fl: (kj[n], 0)
    stat = pltpu.VMEM((HEADS, tq, LANES), F32)
    return pl.pallas_call(
        functools.partial(_attn_prompt_kernel, tq=tq),
        grid_spec=pltpu.PrefetchScalarGridSpec(
            num_scalar_prefetch=3,
            grid=(nsteps,),
            in_specs=[
                pl.BlockSpec((tq, d), qrow),
                pl.BlockSpec((tq, d), krow),
                pl.BlockSpec((tq, d), krow),
                pl.BlockSpec((tq, LANES), qrow),
                pl.BlockSpec((HEADS, tq), lambda n, qi, kj, fl: (0, kj[n])),
            ],
            out_specs=pl.BlockSpec((tq, d), qrow),
            scratch_shapes=[stat, stat, pltpu.VMEM((tq, d), F32), stat],
        ),
        out_shape=jax.ShapeDtypeStruct((s, d), BF16),
        compiler_params=_params(("arbitrary",)),
        name="attn_prompt",
    )(qi, kj, flags, q, kb, vb, c2, ct2)


def _softmax_step(s, h, v_h, m_ref, l_ref, acc_ref):
    sl = slice(h * HEAD_DIM, (h + 1) * HEAD_DIM)
    m_prev = m_ref[h]
    m_new = jnp.maximum(m_prev, jnp.max(s, axis=-1, keepdims=True))
    a = jnp.exp2(m_prev - m_new)
    p = jnp.exp2(s - m_new[:, :1])
    l_ref[h] = a * l_ref[h] + jnp.sum(p, axis=-1, keepdims=True)
    acc_ref[:, sl] = a * acc_ref[:, sl] + jnp.dot(p.astype(BF16), v_h, preferred_element_type=F32)
    m_ref[h] = m_new


def _attn_sample_kernel(q_ref, kc_ref, vc_ref, kn_ref, vn_ref, cq_ref, ctc_ref, ctn_ref, o_ref,
                        m_ref, l_ref, acc_ref, *, tp):
    j = pl.program_id(1)
    tnew = q_ref.shape[0]

    @pl.when(j == 0)
    def _():
        m_ref[...] = jnp.full_like(m_ref, NEG_BIG)
        l_ref[...] = jnp.zeros_like(l_ref)
        acc_ref[...] = jnp.zeros_like(acc_ref)

    for h in range(HEADS):
        sl = slice(h * HEAD_DIM, (h + 1) * HEAD_DIM)
        k_h = kc_ref[pl.ds(h, tp, stride=HEADS), :].astype(BF16)
        v_h = vc_ref[pl.ds(h, tp, stride=HEADS), :].astype(BF16)
        s = _qk(q_ref[:, sl], k_h) + cq_ref[:, h:h + 1] - ctc_ref[h:h + 1, :]
        _softmax_step(s, h, v_h, m_ref, l_ref, acc_ref)

    @pl.when(j == pl.num_programs(1) - 1)
    def _():
        rows = lax.broadcasted_iota(jnp.int32, (tnew, tnew), 0)
        cols = lax.broadcasted_iota(jnp.int32, (tnew, tnew), 1)
        visible = cols <= rows
        for h in range(HEADS):
            sl = slice(h * HEAD_DIM, (h + 1) * HEAD_DIM)
            s = _qk(q_ref[:, sl], kn_ref[:, sl]) + cq_ref[:, h:h + 1] - ctn_ref[h:h + 1, :]
            s = jnp.where(visible, s, NEG_BIG)
            _softmax_step(s, h, vn_ref[:, sl], m_ref, l_ref, acc_ref)
            o_ref[:, sl] = (acc_ref[:, sl] / l_ref[h]).astype(BF16)


def _attn_sample(q, k_cache, v_cache, kb_new, vb_new, c_new, ct_cache, ct_new, *, tnew, tp=512):
    t, d = q.shape
    nb, past_rows, _ = k_cache.shape
    new_rows = pl.BlockSpec((tnew, d), lambda b, j: (b, 0))
    cache = pl.BlockSpec((None, tp * HEADS, HEAD_DIM), lambda b, j: (b, j, 0))
    return pl.pallas_call(
        functools.partial(_attn_sample_kernel, tp=tp),
        grid=(nb, past_rows // (tp * HEADS)),
        in_specs=[
            new_rows, cache, cache, new_rows, new_rows,
            pl.BlockSpec((tnew, LANES), lambda b, j: (b, 0)),
            pl.BlockSpec((None, HEADS, tp), lambda b, j: (b, 0, j)),
            pl.BlockSpec((None, HEADS, tnew), lambda b, j: (b, 0, 0)),
        ],
        out_specs=new_rows,
        out_shape=jax.ShapeDtypeStruct((t, d), BF16),
        scratch_shapes=[pltpu.VMEM((HEADS, tnew, LANES), F32), pltpu.VMEM((HEADS, tnew, LANES), F32),
                        pltpu.VMEM((tnew, d), F32)],
        compiler_params=_params(("parallel", "arbitrary")),
        name="attn_sample",
    )(q, k_cache, v_cache, kb_new, vb_new, c_new, ct_cache, ct_new)


def _oproj_kernel(x_ref, o_ref, wo_ref, g_ref, b_ref, out_ref):
    y = DN_ALPHA * x_ref[...] + jnp.dot(o_ref[...], wo_ref[...], preferred_element_type=F32)
    out_ref[...] = _layer_norm(y, g_ref[...], b_ref[...])


def _oproj_ln(x, o, wo, g, b, *, tm=512):
    t, d = x.shape
    row = pl.BlockSpec((tm, d), lambda i: (i, 0))
    vec = pl.BlockSpec((1, d), lambda i: (0, 0))
    return pl.pallas_call(
        _oproj_kernel,
        grid=(t // tm,),
        in_specs=[row, row, pl.BlockSpec((d, d), lambda i: (0, 0)), vec, vec],
        out_specs=row,
        out_shape=jax.ShapeDtypeStruct((t, d), F32),
        compiler_params=_params(("parallel",)),
        name="oproj_ln",
    )(x, o, wo, g, b)


def kernel(x_prompt, x_sample, state_pool, cache_fox_k, cache_fox_v, cache_fox_logf, ln_g, ln_b,
           ffn_w1, ffn_w3, ffn_w2, pool_w, pool_scale, fox_w_in, fox_b_f, fox_w_o):
    d = D_MODEL
    _, seq, _ = x_prompt.shape
    nb, tnew, _ = x_sample.shape
    past = cache_fox_k.shape[2]
    ns = nb * tnew
    w1b, w3b, w2b = ffn_w1.astype(BF16), ffn_w3.astype(BF16), ffn_w2.astype(BF16)

    def ffn(xp, xs, i, s, ln_idx):
        args = (w1b[i, s], w3b[i, s], w2b[i, s], ln_g[i, ln_idx][None], ln_b[i, ln_idx][None])
        return _ffn_ln(xp, *args, tm=1024), _ffn_ln(xs, *args, tm=ns)

    xp, xs = x_prompt.reshape(seq, d), x_sample.reshape(ns, d)

    xp, xs = ffn(xp, xs, 0, 0, 0)
    xp3, xs3 = xp.reshape(1, seq, d), xs.reshape(nb, tnew, d)
    pool_prompt = xp3[:, seq - POOL_STATE:][None]
    pool_sample = jnp.concatenate([state_pool[0], xs3], axis=1)[:, -POOL_STATE:][None]
    pw = pool_w[0].astype(BF16)
    ps, g1, b1 = pool_scale[0][None], ln_g[0, 1][None], ln_b[0, 1][None]
    tm = 512
    xp = _pool_ln(xp3, xp3, lambda bi, i: (bi, jnp.maximum(i * (tm // HALO) - 1, 0), 0), pw, ps, g1, b1,
                  bb=1, tm=tm, start_pos=0, zero_first=True).reshape(seq, d)
    prev = jnp.pad(state_pool[0], ((0, 0), (HALO - POOL_STATE, 0), (0, 0)))
    xs = _pool_ln(xs3, prev, lambda bi, i: (bi, 0, 0), pw, ps, g1, b1,
                  bb=nb, tm=tnew, start_pos=past, zero_first=False).reshape(ns, d)
    xp, xs = ffn(xp, xs, 0, 1, 2)

    xp, xs = ffn(xp, xs, 1, 0, 0)
    w_in = fox_w_in[0]
    wq, wk, wv = (w_in[:, n * d:(n + 1) * d].astype(BF16) for n in range(3))
    wf = jnp.pad(w_in[:, 3 * d:], ((0, 0), (0, LANES - HEADS))).astype(BF16)
    bf = jnp.pad(fox_b_f[0], (0, LANES - HEADS))[None]
    q_p, k_p, v_p, kb_p, vb_p, lf_p, qmax, kmax = _fox_proj(xp, wq, wk, wv, wf, bf)
    q_s, k_s, v_s, kb_s, vb_s, lf_s, _, _ = _fox_proj(xs, wq, wk, wv, wf, bf)

    c_p = _cumsum_time(lf_p[None], tc=512)[0]
    o_p = _attn_prompt(q_p, kb_p, vb_p, c_p, c_p[:, :HEADS].T, qmax[:, 0, :HEADS], kmax[:, 0, :HEADS])

    lf_cache = jnp.pad(cache_fox_logf[0], ((0, 0), (0, 0), (0, LANES - HEADS)))
    lf_all = jnp.concatenate([lf_cache, lf_s.reshape(nb, tnew, LANES)], axis=1)
    c_s = _cumsum_time(lf_all, tc=(past + tnew) // 3)
    ct_s = c_s[:, :, :HEADS].transpose(0, 2, 1)
    o_s = _attn_sample(q_s, cache_fox_k[0].reshape(nb, past * HEADS, HEAD_DIM),
                       cache_fox_v[0].reshape(nb, past * HEADS, HEAD_DIM), kb_s, vb_s,
                       c_s[:, past:].reshape(ns, LANES), ct_s[:, :, :past], ct_s[:, :, past:], tnew=tnew)

    wo = fox_w_o[0].astype(BF16)
    g1, b1 = ln_g[1, 1][None], ln_b[1, 1][None]
    xp, xs = _oproj_ln(xp, o_p, wo, g1, b1), _oproj_ln(xs, o_s, wo, g1, b1)
    xp, xs = ffn(xp, xs, 1, 1, 2)

    shp = (HEADS, HEAD_DIM)
    return (xp.reshape(1, seq, d), xs.reshape(nb, tnew, d), pool_prompt, pool_sample,
            k_p.reshape(1, 1, seq, *shp), v_p.reshape(1, 1, seq, *shp), lf_p[:, :HEADS].reshape(1, 1, seq, HEADS),
            k_s.reshape(1, nb, tnew, *shp), v_s.reshape(1, nb, tnew, *shp),
            lf_s[:, :HEADS].reshape(1, nb, tnew, HEADS))
```

```python
import functools
import math

import jax
import jax.numpy as jnp
from jax import lax
from jax.experimental import pallas as pl
from jax.experimental.pallas import tpu as pltpu

F32 = jnp.float32
BF16 = jnp.bfloat16

D_MODEL = 2048
DEPTH = 2
POOL_WINDOWS = (2, 4, 8, 16)
POOL_GROUP = D_MODEL // len(POOL_WINDOWS)
POOL_STATE = max(POOL_WINDOWS) - 1
HALO = POOL_STATE + 1
HEAD_DIM = 128
HEADS = D_MODEL // HEAD_DIM
LN_EPS = 1e-5
DN_ALPHA = (2 * DEPTH) ** 0.25
LOG2E = math.log2(math.e)
Q_SCALE = HEAD_DIM ** -0.5 * LOG2E
NEG_BIG = -1e30
SKIP_LOG2 = 150.0
LANES = 128
SUBLANES = 8
VMEM_LIMIT = 60 * 1024 * 1024


def _params(semantics):
    return pltpu.CompilerParams(dimension_semantics=semantics, vmem_limit_bytes=VMEM_LIMIT)


def _layer_norm(y, g, b):
    mu = jnp.mean(y, axis=-1, keepdims=True)
    yc = y - mu
    var = jnp.mean(yc * yc, axis=-1, keepdims=True)
    return yc * lax.rsqrt(var + LN_EPS) * g + b


def _ffn_kernel(x_ref, w1_ref, w3_ref, w2_ref, g_ref, b_ref, o_ref, xb_ref):
    j = pl.program_id(1)

    @pl.when(j == 0)
    def _():
        xb_ref[...] = x_ref[...].astype(BF16)
        o_ref[...] = jnp.zeros_like(o_ref)

    xb = xb_ref[...]
    h1 = jnp.dot(xb, w1_ref[...], preferred_element_type=F32)
    h3 = jnp.dot(xb, w3_ref[...], preferred_element_type=F32)
    gate = (h1 * jax.nn.sigmoid(h1) * h3).astype(BF16)
    o_ref[...] += jnp.dot(gate, w2_ref[...], preferred_element_type=F32)

    @pl.when(j == pl.num_programs(1) - 1)
    def _():
        y = DN_ALPHA * x_ref[...] + 0.5 * o_ref[...]
        o_ref[...] = _layer_norm(y, g_ref[...], b_ref[...])


def _ffn_ln(x, w1, w3, w2, layer, half, g, b, *, tm, tf=512):
    t, d = x.shape
    f = w1.shape[-1]
    return pl.pallas_call(
        _ffn_kernel,
        grid=(t // tm, f // tf),
        in_specs=[
            pl.BlockSpec((tm, d), lambda i, j: (i, 0), pipeline_mode=pl.Buffered(1)),
            pl.BlockSpec((None, None, d, tf), lambda i, j: (layer, half, 0, j)),
            pl.BlockSpec((None, None, d, tf), lambda i, j: (layer, half, 0, j)),
            pl.BlockSpec((None, None, tf, d), lambda i, j: (layer, half, j, 0)),
            pl.BlockSpec((1, d), lambda i, j: (0, 0)),
            pl.BlockSpec((1, d), lambda i, j: (0, 0)),
        ],
        out_specs=pl.BlockSpec((tm, d), lambda i, j: (i, 0)),
        out_shape=jax.ShapeDtypeStruct((t, d), F32),
        scratch_shapes=[pltpu.VMEM((tm, d), BF16)],
        compiler_params=_params(("parallel", "arbitrary")),
        name="ffn_ln",
    )(x, w1, w3, w2, g, b)


def _pool_kernel(x_ref, halo_ref, pw_ref, ps_ref, g_ref, b_ref, o_ref, buf_ref, *, tm, start_pos, zero_first):
    i = pl.program_id(1)
    bb = x_ref.shape[0]
    x = x_ref[...]
    halo = halo_ref[...]
    if zero_first:
        halo = jnp.where(i == 0, 0.0, halo)
    pos = start_pos + i * tm + lax.broadcasted_iota(jnp.int32, (1, tm, 1), 1)
    ys = []
    for g, w in enumerate(POOL_WINDOWS):
        cols = slice(g * POOL_GROUP, (g + 1) * POOL_GROUP)
        xg = x[:, :, cols]
        buf_ref[:, 0:HALO, :] = halo[:, :, cols]
        buf_ref[:, HALO:, :] = xg
        win = xg
        for k in range(1, w):
            win = win + buf_ref[:, HALO - k:HALO - k + tm, :]
        cnt = jnp.minimum(pos + 1, w).astype(F32)
        diff = (win / cnt - xg).reshape(bb * tm, POOL_GROUP)
        ys.append(jnp.dot(diff.astype(BF16), pw_ref[g], preferred_element_type=F32))
    y = jnp.concatenate(ys, axis=-1) * ps_ref[...]
    out = _layer_norm(DN_ALPHA * x.reshape(bb * tm, D_MODEL) + y, g_ref[...], b_ref[...])
    o_ref[...] = out.reshape(bb, tm, D_MODEL)


def _pool_ln(x, halo_src, halo_map, pw, ps, g, b, *, bb, tm, start_pos, zero_first):
    nb, t, d = x.shape
    kern = functools.partial(_pool_kernel, tm=tm, start_pos=start_pos, zero_first=zero_first)
    return pl.pallas_call(
        kern,
        grid=(nb // bb, t // tm),
        in_specs=[
            pl.BlockSpec((bb, tm, d), lambda bi, i: (bi, i, 0)),
            pl.BlockSpec((bb, HALO, d), halo_map),
            pl.BlockSpec((len(POOL_WINDOWS), POOL_GROUP, POOL_GROUP), lambda bi, i: (0, 0, 0)),
            pl.BlockSpec((1, d), lambda bi, i: (0, 0)),
            pl.BlockSpec((1, d), lambda bi, i: (0, 0)),
            pl.BlockSpec((1, d), lambda bi, i: (0, 0)),
        ],
        out_specs=pl.BlockSpec((bb, tm, d), lambda bi, i: (bi, i, 0)),
        out_shape=jax.ShapeDtypeStruct((nb, t, d), F32),
        scratch_shapes=[pltpu.VMEM((bb, HALO + tm, POOL_GROUP), F32)],
        compiler_params=_params(("parallel", "arbitrary")),
        name="pool_ln",
    )(x, halo_src, pw, ps, g, b)


def _max_sq_norm_per_head(xb, first_head, acc):
    sq = xb.astype(F32)
    sq = sq * sq
    lane = lax.broadcasted_iota(jnp.int32, acc.shape, 1)
    for hh in range(xb.shape[1] // HEAD_DIM):
        row = jnp.sum(sq[:, hh * HEAD_DIM:(hh + 1) * HEAD_DIM], axis=-1, keepdims=True)
        acc = jnp.where(lane == first_head + hh, jnp.max(row, axis=0, keepdims=True), acc)
    return acc


def _proj_kernel(x_ref, wq_ref, wk_ref, wv_ref, wf_ref, bf_ref,
                 q_ref, k_ref, v_ref, kb_ref, vb_ref, lf_ref, qn_ref, kn_ref, xb_ref):
    j = pl.program_id(1)
    heads_per_step = q_ref.shape[1] // HEAD_DIM

    @pl.when(j == 0)
    def _():
        xb = x_ref[...].astype(BF16)
        xb_ref[...] = xb
        fl = jnp.dot(xb, wf_ref[...], preferred_element_type=F32) + bf_ref[...]
        lf_ref[...] = jnp.minimum(fl, 0.0) - jnp.log1p(jnp.exp(-jnp.abs(fl)))
        qn_ref[...] = jnp.zeros_like(qn_ref)
        kn_ref[...] = jnp.zeros_like(kn_ref)

    xb = xb_ref[...]
    qb = (jnp.dot(xb, wq_ref[...], preferred_element_type=F32) * Q_SCALE).astype(BF16)
    q_ref[...] = qb
    k = jnp.dot(xb, wk_ref[...], preferred_element_type=F32)
    kb = k.astype(BF16)
    kb_ref[...] = kb
    v = jnp.dot(xb, wv_ref[...], preferred_element_type=F32)
    vb_ref[...] = v.astype(BF16)
    tm = x_ref.shape[0]
    for hh in range(heads_per_step):
        rows = pl.ds(j * heads_per_step + hh, tm, stride=HEADS)
        k_ref[rows, :] = k[:, hh * HEAD_DIM:(hh + 1) * HEAD_DIM]
        v_ref[rows, :] = v[:, hh * HEAD_DIM:(hh + 1) * HEAD_DIM]
    qn_ref[...] = _max_sq_norm_per_head(qb, j * heads_per_step, qn_ref[...])
    kn_ref[...] = _max_sq_norm_per_head(kb, j * heads_per_step, kn_ref[...])


def _fox_proj(x, w_in, wf, bf, *, tm=512, tn=512):
    t, d = x.shape
    nd = d // tn
    row = pl.BlockSpec((tm, tn), lambda i, j: (i, j))

    def wcols(part):
        return pl.BlockSpec((d, tn), lambda i, j: (0, part * nd + j))

    wq, wk, wv = wcols(0), wcols(1), wcols(2)
    tile_stat = pl.BlockSpec((None, SUBLANES, LANES), lambda i, j: (i, 0, 0))
    by_head = pl.BlockSpec((tm * HEADS, HEAD_DIM), lambda i, j: (i, 0))
    return pl.pallas_call(
        _proj_kernel,
        grid=(t // tm, d // tn),
        in_specs=[
            pl.BlockSpec((tm, d), lambda i, j: (i, 0)),
            wq, wk, wv,
            pl.BlockSpec((d, LANES), lambda i, j: (0, 0)),
            pl.BlockSpec((1, LANES), lambda i, j: (0, 0)),
        ],
        out_specs=[row, by_head, by_head, row, row, pl.BlockSpec((tm, LANES), lambda i, j: (i, 0)), tile_stat, tile_stat],
        out_shape=[
            jax.ShapeDtypeStruct((t, d), BF16),
            jax.ShapeDtypeStruct((t * HEADS, HEAD_DIM), F32),
            jax.ShapeDtypeStruct((t * HEADS, HEAD_DIM), F32),
            jax.ShapeDtypeStruct((t, d), BF16),
            jax.ShapeDtypeStruct((t, d), BF16),
            jax.ShapeDtypeStruct((t, LANES), F32),
            jax.ShapeDtypeStruct((t // tm, SUBLANES, LANES), F32),
            jax.ShapeDtypeStruct((t // tm, SUBLANES, LANES), F32),
        ],
        scratch_shapes=[pltpu.VMEM((tm, d), BF16)],
        compiler_params=_params(("parallel", "arbitrary")),
        name="fox_proj",
    )(x, w_in, w_in, w_in, wf, bf)


def _cumsum_kernel(x_ref, o_ref, carry_ref, *, tc):
    @pl.when(pl.program_id(1) == 0)
    def _():
        carry_ref[...] = jnp.zeros_like(carry_ref)

    x = x_ref[...]
    r = lax.broadcasted_iota(jnp.int32, (tc, tc), 0)
    c = lax.broadcasted_iota(jnp.int32, (tc, tc), 1)
    tri = (c <= r).astype(BF16)
    hi = x.astype(BF16)
    rem = x - hi.astype(F32)
    mid = rem.astype(BF16)
    lo = (rem - mid.astype(F32)).astype(BF16)
    cs = (jnp.dot(tri, hi, preferred_element_type=F32)
          + jnp.dot(tri, mid, preferred_element_type=F32)
          + jnp.dot(tri, lo, preferred_element_type=F32)) + carry_ref[...]
    o_ref[...] = cs * LOG2E
    carry_ref[...] = cs[tc - 1:tc, :]


def _cumsum_time(x, *, tc):
    nb, t, w = x.shape
    return pl.pallas_call(
        functools.partial(_cumsum_kernel, tc=tc),
        grid=(nb, t // tc),
        in_specs=[pl.BlockSpec((None, tc, w), lambda b, j: (b, j, 0))],
        out_specs=pl.BlockSpec((None, tc, w), lambda b, j: (b, j, 0)),
        out_shape=jax.ShapeDtypeStruct((nb, t, w), F32),
        scratch_shapes=[pltpu.VMEM((1, w), F32)],
        compiler_params=_params(("parallel", "arbitrary")),
        name="cumsum_time",
    )(x)


def _qk(q_h, k_h):
    return lax.dot_general(q_h, k_h, (((1,), (1,)), ((), ())), preferred_element_type=F32)


def _attn_schedule(qmax, kmax, c2, tq, nsteps):
    nq = qmax.shape[0]
    qn, kn = jnp.sqrt(qmax), jnp.sqrt(kmax)
    c_first, c_last = c2[0::tq, :HEADS], c2[tq - 1::tq, :HEADS]
    bound = qn[:, None, :] * (kn[None, :, :] + kn[:, None, :]) + c_first[:, None, :] - c_last[None, :, :]
    tile = jnp.arange(nq, dtype=jnp.int32)
    skip = jnp.all(bound < -SKIP_LOG2, axis=-1) & (tile[None, :] < tile[:, None])
    jstart = jnp.argmin(skip.astype(jnp.int32), axis=1).astype(jnp.int32)
    cnt = tile - jstart + 1
    ends = jnp.cumsum(cnt)
    starts = ends - cnt
    n = jnp.arange(nsteps, dtype=jnp.int32)
    valid = n < ends[-1]
    qi = jnp.minimum(jnp.sum((ends[None, :] <= n[:, None]).astype(jnp.int32), axis=1), nq - 1)
    kj = jstart[qi] + n - starts[qi]
    qi = jnp.where(valid, qi, nq - 1)
    kj = jnp.where(valid, kj, nq - 1)
    first = valid & (kj == jstart[qi])
    return qi, kj, valid.astype(jnp.int32) + 2 * first.astype(jnp.int32)


def _attn_prompt_kernel(qi_ref, kj_ref, fl_ref, q_ref, k_ref, v_ref, c_ref, ct_ref, o_ref,
                        m_ref, l_ref, acc_ref, crep_ref, *, tq):
    n = pl.program_id(0)
    flags = fl_ref[n]
    nchunk = tq // LANES

    @pl.when(flags >= 2)
    def _():
        m_ref[...] = jnp.full_like(m_ref, NEG_BIG)
        l_ref[...] = jnp.zeros_like(l_ref)
        acc_ref[...] = jnp.zeros_like(acc_ref)
        for h in range(HEADS):
            crep_ref[h] = jnp.broadcast_to(c_ref[:, h:h + 1], (tq, LANES))

    def sweep(masked):
        if masked:
            row = lax.broadcasted_iota(jnp.int32, (tq, LANES), 0)
            lane = lax.broadcasted_iota(jnp.int32, (tq, LANES), 1)
        for h in range(HEADS):
            sl = slice(h * HEAD_DIM, (h + 1) * HEAD_DIM)
            s = _qk(q_ref[:, sl], k_ref[:, sl])
            crep = crep_ref[h]
            chunks = []
            for c in range(nchunk):
                cs = slice(c * LANES, (c + 1) * LANES)
                sc = s[:, cs] + crep - ct_ref[h:h + 1, cs]
                if masked:
                    sc = jnp.where(lane + c * LANES <= row, sc, NEG_BIG)
                chunks.append(sc)
            mx = chunks[0]
            for sc in chunks[1:]:
                mx = jnp.maximum(mx, sc)
            m_prev = m_ref[h]
            m_new = jnp.maximum(m_prev, jnp.max(mx, axis=-1, keepdims=True))
            a = jnp.exp2(m_prev - m_new)
            ps = [jnp.exp2(sc - m_new) for sc in chunks]
            lsum = ps[0]
            for p in ps[1:]:
                lsum = lsum + p
            l_ref[h] = a * l_ref[h] + lsum
            p = jnp.concatenate(ps, axis=-1).astype(BF16)
            acc_ref[:, sl] = a * acc_ref[:, sl] + jnp.dot(p, v_ref[:, sl], preferred_element_type=F32)
            m_ref[h] = m_new

    diagonal = kj_ref[n] == qi_ref[n]

    @pl.when((flags >= 1) & jnp.logical_not(diagonal))
    def _():
        sweep(False)

    @pl.when((flags >= 1) & diagonal)
    def _():
        sweep(True)
        for h in range(HEADS):
            sl = slice(h * HEAD_DIM, (h + 1) * HEAD_DIM)
            l_tot = jnp.sum(l_ref[h], axis=-1, keepdims=True)
            o_ref[:, sl] = (acc_ref[:, sl] / l_tot).astype(BF16)


def _attn_prompt(q, kb, vb, c2, ct2, qmax, kmax, *, tq=512):
    s, d = q.shape
    nq = s // tq
    nsteps = nq * (nq + 1) // 2
    qi, kj, flags = _attn_schedule(qmax, kmax, c2, tq, nsteps)
    qrow = lambda n, qi, kj, fl: (qi[n], 0)
    krow = lambda n, qi, kj, fl: (kj[n], 0)
    stat = pltpu.VMEM((HEADS, tq, LANES), F32)
    return pl.pallas_call(
        functools.partial(_attn_prompt_kernel, tq=tq),
        grid_spec=pltpu.PrefetchScalarGridSpec(
            num_scalar_prefetch=3,
            grid=(nsteps,),
            in_specs=[
                pl.BlockSpec((tq, d), qrow),
                pl.BlockSpec((tq, d), krow),
                pl.BlockSpec((tq, d), krow),
                pl.BlockSpec((tq, LANES), qrow),
                pl.BlockSpec((HEADS, tq), lambda n, qi, kj, fl: (0, kj[n])),
            ],
            out_specs=pl.BlockSpec((tq, d), qrow),
            scratch_shapes=[stat, stat, pltpu.VMEM((tq, d), F32), stat],
        ),
        out_shape=jax.ShapeDtypeStruct((s, d), BF16),
        compiler_params=_params(("arbitrary",)),
        name="attn_prompt",
    )(qi, kj, flags, q, kb, vb, c2, ct2)


def _attn_sample_kernel(q_ref, kc_ref, vc_ref, kn_ref, vn_ref, cq_ref, ctc_ref, ctn_ref, o_ref,
                        m_ref, l_ref, acc_ref, cqr_ref, *, tp):
    j = pl.program_id(1)
    tnew = q_ref.shape[0]
    heads = [slice(h * HEAD_DIM, (h + 1) * HEAD_DIM) for h in range(HEADS)]

    @pl.when(j == 0)
    def _():
        m_ref[...] = jnp.full_like(m_ref, NEG_BIG)
        l_ref[...] = jnp.zeros_like(l_ref)
        acc_ref[...] = jnp.zeros_like(acc_ref)
        for h in range(HEADS):
            cqr_ref[h * tnew:(h + 1) * tnew, :] = jnp.broadcast_to(cq_ref[:, h:h + 1], (tnew, LANES))

    def update(keys, values, key_c2, visible):
        n = key_c2.shape[1]
        s = jnp.concatenate([_qk(q_ref[:, heads[h]], keys[h]) for h in range(HEADS)], axis=0)
        c_keys = jnp.concatenate([jnp.broadcast_to(key_c2[h:h + 1, :], (tnew, n)) for h in range(HEADS)], axis=0)
        s = s + cqr_ref[:, :1] - c_keys
        if visible is not None:
            s = jnp.where(visible, s, NEG_BIG)
        m_prev = m_ref[...]
        m_new = jnp.maximum(m_prev, jnp.max(s, axis=-1, keepdims=True))
        a = jnp.exp2(m_prev - m_new)
        p = jnp.exp2(s - m_new[:, :1])
        l_ref[...] = a * l_ref[...] + jnp.sum(p, axis=-1, keepdims=True)
        pb = p.astype(BF16)
        pv = jnp.concatenate([jnp.dot(pb[h * tnew:(h + 1) * tnew], values[h], preferred_element_type=F32)
                              for h in range(HEADS)], axis=0)
        acc_ref[...] = a * acc_ref[...] + pv
        m_ref[...] = m_new

    update([kc_ref[pl.ds(h, tp, stride=HEADS), :].astype(BF16) for h in range(HEADS)],
           [vc_ref[pl.ds(h, tp, stride=HEADS), :].astype(BF16) for h in range(HEADS)],
           ctc_ref[...], None)

    @pl.when(j == pl.num_programs(1) - 1)
    def _():
        query = lax.broadcasted_iota(jnp.int32, (HEADS * tnew, tnew), 0) % tnew
        key = lax.broadcasted_iota(jnp.int32, (HEADS * tnew, tnew), 1)
        update([kn_ref[:, sl] for sl in heads], [vn_ref[:, sl] for sl in heads], ctn_ref[...], key <= query)
        out = acc_ref[...] / l_ref[...]
        for h in range(HEADS):
            o_ref[:, heads[h]] = out[h * tnew:(h + 1) * tnew, :].astype(BF16)


def _attn_sample(q, k_cache, v_cache, kb_new, vb_new, c_new, ct_cache, ct_new, *, tnew, tp=512):
    t, d = q.shape
    nb, past_rows, _ = k_cache.shape
    new_rows = pl.BlockSpec((tnew, d), lambda b, j: (b, 0))
    cache = pl.BlockSpec((None, tp * HEADS, HEAD_DIM), lambda b, j: (b, j, 0))
    return pl.pallas_call(
        functools.partial(_attn_sample_kernel, tp=tp),
        grid=(nb, past_rows // (tp * HEADS)),
        in_specs=[
            new_rows, cache, cache, new_rows, new_rows,
            pl.BlockSpec((tnew, LANES), lambda b, j: (b, 0)),
            pl.BlockSpec((None, HEADS, tp), lambda b, j: (b, 0, j)),
            pl.BlockSpec((None, HEADS, tnew), lambda b, j: (b, 0, 0)),
        ],
        out_specs=new_rows,
        out_shape=jax.ShapeDtypeStruct((t, d), BF16),
        scratch_shapes=[pltpu.VMEM((HEADS * tnew, LANES), F32), pltpu.VMEM((HEADS * tnew, LANES), F32),
                        pltpu.VMEM((HEADS * tnew, HEAD_DIM), F32), pltpu.VMEM((HEADS * tnew, LANES), F32)],
        compiler_params=_params(("parallel", "arbitrary")),
        name="attn_sample",
    )(q, k_cache, v_cache, kb_new, vb_new, c_new, ct_cache, ct_new)


def _oproj_kernel(x_ref, o_ref, wo_ref, g_ref, b_ref, out_ref):
    y = DN_ALPHA * x_ref[...] + jnp.dot(o_ref[...], wo_ref[...], preferred_element_type=F32)
    out_ref[...] = _layer_norm(y, g_ref[...], b_ref[...])


def _oproj_ln(x, o, wo, g, b, *, tm=512):
    t, d = x.shape
    row = pl.BlockSpec((tm, d), lambda i: (i, 0))
    vec = pl.BlockSpec((1, d), lambda i: (0, 0))
    return pl.pallas_call(
        _oproj_kernel,
        grid=(t // tm,),
        in_specs=[row, row, pl.BlockSpec((d, d), lambda i: (0, 0)), vec, vec],
        out_specs=row,
        out_shape=jax.ShapeDtypeStruct((t, d), F32),
        compiler_params=_params(("parallel",)),
        name="oproj_ln",
    )(x, o, wo, g, b)


def kernel(x_prompt, x_sample, state_pool, cache_fox_k, cache_fox_v, cache_fox_logf, ln_g, ln_b,
           ffn_w1, ffn_w3, ffn_w2, pool_w, pool_scale, fox_w_in, fox_b_f, fox_w_o):
    d = D_MODEL
    _, seq, _ = x_prompt.shape
    nb, tnew, _ = x_sample.shape
    past = cache_fox_k.shape[2]
    ns = nb * tnew
    w1b, w3b, w2b = ffn_w1.astype(BF16), ffn_w3.astype(BF16), ffn_w2.astype(BF16)

    def ffn(xp, xs, i, s, ln_idx):
        args = (w1b, w3b, w2b, i, s, ln_g[i, ln_idx][None], ln_b[i, ln_idx][None])
        return _ffn_ln(xp, *args, tm=1024), _ffn_ln(xs, *args, tm=ns)

    xp, xs = x_prompt.reshape(seq, d), x_sample.reshape(ns, d)

    xp, xs = ffn(xp, xs, 0, 0, 0)
    xp3, xs3 = xp.reshape(1, seq, d), xs.reshape(nb, tnew, d)
    pool_prompt = xp3[:, seq - POOL_STATE:][None]
    pool_sample = jnp.concatenate([state_pool[0], xs3], axis=1)[:, -POOL_STATE:][None]
    pw = pool_w[0].astype(BF16)
    ps, g1, b1 = pool_scale[0][None], ln_g[0, 1][None], ln_b[0, 1][None]
    tm = 512
    xp = _pool_ln(xp3, xp3, lambda bi, i: (bi, jnp.maximum(i * (tm // HALO) - 1, 0), 0), pw, ps, g1, b1,
                  bb=1, tm=tm, start_pos=0, zero_first=True).reshape(seq, d)
    prev = jnp.pad(state_pool[0], ((0, 0), (HALO - POOL_STATE, 0), (0, 0)))
    xs = _pool_ln(xs3, prev, lambda bi, i: (bi, 0, 0), pw, ps, g1, b1,
                  bb=nb, tm=tnew, start_pos=past, zero_first=False).reshape(ns, d)
    xp, xs = ffn(xp, xs, 0, 1, 2)

    xp, xs = ffn(xp, xs, 1, 0, 0)
    w_in = fox_w_in[0].astype(BF16)
    wf = jnp.pad(w_in[:, 3 * d:], ((0, 0), (0, LANES - HEADS)))
    bf = jnp.pad(fox_b_f[0], (0, LANES - HEADS))[None]
    q_p, k_p, v_p, kb_p, vb_p, lf_p, qmax, kmax = _fox_proj(xp, w_in, wf, bf)
    q_s, k_s, v_s, kb_s, vb_s, lf_s, _, _ = _fox_proj(xs, w_in, wf, bf)

    c_p = _cumsum_time(lf_p[None], tc=512)[0]
    o_p = _attn_prompt(q_p, kb_p, vb_p, c_p, c_p[:, :HEADS].T, qmax[:, 0, :HEADS], kmax[:, 0, :HEADS])

    lf_cache = jnp.pad(cache_fox_logf[0], ((0, 0), (0, 0), (0, LANES - HEADS)))
    lf_all = jnp.concatenate([lf_cache, lf_s.reshape(nb, tnew, LANES)], axis=1)
    c_s = _cumsum_time(lf_all, tc=(past + tnew) // 3)
    ct_s = c_s[:, :, :HEADS].transpose(0, 2, 1)
    o_s = _attn_sample(q_s, cache_fox_k[0].reshape(nb, past * HEADS, HEAD_DIM),
                       cache_fox_v[0].reshape(nb, past * HEADS, HEAD_DIM), kb_s, vb_s,
                       c_s[:, past:].reshape(ns, LANES), ct_s[:, :, :past], ct_s[:, :, past:], tnew=tnew)

    wo = fox_w_o[0].astype(BF16)
    g1, b1 = ln_g[1, 1][None], ln_b[1, 1][None]
    xp, xs = _oproj_ln(xp, o_p, wo, g1, b1), _oproj_ln(xs, o_s, wo, g1, b1)
    xp, xs = ffn(xp, xs, 1, 1, 2)

    shp = (HEADS, HEAD_DIM)
    return (xp.reshape(1, seq, d), xs.reshape(nb, tnew, d), pool_prompt, pool_sample,
            k_p.reshape(1, 1, seq, *shp), v_p.reshape(1, 1, seq, *shp), lf_p[:, :HEADS].reshape(1, 1, seq, HEADS),
            k_s.reshape(1, nb, tnew, *shp), v_s.reshape(1, nb, tnew, *shp),
            lf_s[:, :HEADS].reshape(1, nb, tnew, HEADS))
```

```python
import functools
import math

import jax
import jax.numpy as jnp
import numpy as np
from jax import lax
from jax.experimental import pallas as pl
from jax.experimental.pallas import tpu as pltpu

F32 = jnp.float32
BF16 = jnp.bfloat16

D_MODEL = 2048
DEPTH = 2
POOL_WINDOWS = (2, 4, 8, 16)
POOL_GROUP = D_MODEL // len(POOL_WINDOWS)
POOL_STATE = max(POOL_WINDOWS) - 1
HALO = POOL_STATE + 1
HEAD_DIM = 128
HEADS = D_MODEL // HEAD_DIM
LN_EPS = 1e-5
DN_ALPHA = (2 * DEPTH) ** 0.25
LOG2E = math.log2(math.e)
Q_SCALE = HEAD_DIM ** -0.5 * LOG2E
NEG_BIG = -1e30
SKIP_LOG2 = 150.0
LANES = 128
SUBLANES = 8
VMEM_LIMIT = 60 * 1024 * 1024


def _params(semantics):
    return pltpu.CompilerParams(dimension_semantics=semantics, vmem_limit_bytes=VMEM_LIMIT)


def _layer_norm(y, g, b):
    mu = jnp.mean(y, axis=-1, keepdims=True)
    yc = y - mu
    var = jnp.mean(yc * yc, axis=-1, keepdims=True)
    return yc * lax.rsqrt(var + LN_EPS) * g + b


def _ffn_kernel(x_ref, w1_ref, w3_ref, w2_ref, g_ref, b_ref, o_ref, xb_ref):
    j = pl.program_id(1)

    @pl.when(j == 0)
    def _():
        xb_ref[...] = x_ref[...].astype(BF16)
        o_ref[...] = jnp.zeros_like(o_ref)

    xb = xb_ref[...]
    h1 = jnp.dot(xb, w1_ref[...], preferred_element_type=F32)
    h3 = jnp.dot(xb, w3_ref[...], preferred_element_type=F32)
    gate = (h1 * jax.nn.sigmoid(h1) * h3).astype(BF16)
    o_ref[...] += jnp.dot(gate, w2_ref[...], preferred_element_type=F32)

    @pl.when(j == pl.num_programs(1) - 1)
    def _():
        y = DN_ALPHA * x_ref[...] + 0.5 * o_ref[...]
        o_ref[...] = _layer_norm(y, g_ref[...], b_ref[...])


def _ffn_ln(x, w1, w3, w2, layer, half, g, b, *, tm, tf):
    t, d = x.shape
    f = w1.shape[-1]
    return pl.pallas_call(
        _ffn_kernel,
        grid=(t // tm, f // tf),
        in_specs=[
            pl.BlockSpec((tm, d), lambda i, j: (i, 0)),
            pl.BlockSpec((None, None, d, tf), lambda i, j: (layer, half, 0, j)),
            pl.BlockSpec((None, None, d, tf), lambda i, j: (layer, half, 0, j)),
            pl.BlockSpec((None, None, tf, d), lambda i, j: (layer, half, j, 0)),
            pl.BlockSpec((1, d), lambda i, j: (0, 0)),
            pl.BlockSpec((1, d), lambda i, j: (0, 0)),
        ],
        out_specs=pl.BlockSpec((tm, d), lambda i, j: (i, 0)),
        out_shape=jax.ShapeDtypeStruct((t, d), F32),
        scratch_shapes=[pltpu.VMEM((tm, d), BF16)],
        compiler_params=_params(("parallel", "arbitrary")),
        name="ffn_ln",
    )(x, w1, w3, w2, g, b)


def _pool_kernel(x_ref, halo_ref, pw_ref, ps_ref, g_ref, b_ref, o_ref, buf_ref, *, tm, start_pos, zero_first):
    i = pl.program_id(1)
    bb = x_ref.shape[0]
    x = x_ref[...]
    halo = halo_ref[...]
    if zero_first:
        halo = jnp.where(i == 0, 0.0, halo)
    pos = start_pos + i * tm + lax.broadcasted_iota(jnp.int32, (1, tm, 1), 1)
    ys = []
    for g, w in enumerate(POOL_WINDOWS):
        cols = slice(g * POOL_GROUP, (g + 1) * POOL_GROUP)
        xg = x[:, :, cols]
        buf_ref[:, 0:HALO, :] = halo[:, :, cols]
        buf_ref[:, HALO:, :] = xg
        win = xg
        for k in range(1, w):
            win = win + buf_ref[:, HALO - k:HALO - k + tm, :]
        cnt = jnp.minimum(pos + 1, w).astype(F32)
        diff = (win / cnt - xg).reshape(bb * tm, POOL_GROUP)
        ys.append(jnp.dot(diff.astype(BF16), pw_ref[g], preferred_element_type=F32))
    y = jnp.concatenate(ys, axis=-1) * ps_ref[...]
    out = _layer_norm(DN_ALPHA * x.reshape(bb * tm, D_MODEL) + y, g_ref[...], b_ref[...])
    o_ref[...] = out.reshape(bb, tm, D_MODEL)


def _pool_ln(x, halo_src, halo_map, pw, ps, g, b, *, bb, tm, start_pos, zero_first):
    nb, t, d = x.shape
    kern = functools.partial(_pool_kernel, tm=tm, start_pos=start_pos, zero_first=zero_first)
    return pl.pallas_call(
        kern,
        grid=(nb // bb, t // tm),
        in_specs=[
            pl.BlockSpec((bb, tm, d), lambda bi, i: (bi, i, 0)),
            pl.BlockSpec((bb, HALO, d), halo_map),
            pl.BlockSpec((len(POOL_WINDOWS), POOL_GROUP, POOL_GROUP), lambda bi, i: (0, 0, 0)),
            pl.BlockSpec((1, d), lambda bi, i: (0, 0)),
            pl.BlockSpec((1, d), lambda bi, i: (0, 0)),
            pl.BlockSpec((1, d), lambda bi, i: (0, 0)),
        ],
        out_specs=pl.BlockSpec((bb, tm, d), lambda bi, i: (bi, i, 0)),
        out_shape=jax.ShapeDtypeStruct((nb, t, d), F32),
        scratch_shapes=[pltpu.VMEM((bb, HALO + tm, POOL_GROUP), F32)],
        compiler_params=_params(("parallel", "arbitrary")),
        name="pool_ln",
    )(x, halo_src, pw, ps, g, b)


def _max_sq_norm_per_head(xb, first_head, acc):
    sq = xb.astype(F32)
    sq = sq * sq
    lane = lax.broadcasted_iota(jnp.int32, acc.shape, 1)
    for hh in range(xb.shape[1] // HEAD_DIM):
        row = jnp.sum(sq[:, hh * HEAD_DIM:(hh + 1) * HEAD_DIM], axis=-1, keepdims=True)
        acc = jnp.where(lane == first_head + hh, jnp.max(row, axis=0, keepdims=True), acc)
    return acc


def _proj_kernel(x_ref, wq_ref, wk_ref, wv_ref, wf_ref, bf_ref,
                 q_ref, k_ref, v_ref, kb_ref, vb_ref, lf_ref, qn_ref, kn_ref, xb_ref):
    j = pl.program_id(1)
    heads_per_step = q_ref.shape[1] // HEAD_DIM

    @pl.when(j == 0)
    def _():
        xb = x_ref[...].astype(BF16)
        xb_ref[...] = xb
        fl = jnp.dot(xb, wf_ref[...], preferred_element_type=F32) + bf_ref[...]
        lf_ref[...] = jnp.minimum(fl, 0.0) - jnp.log1p(jnp.exp(-jnp.abs(fl)))
        qn_ref[...] = jnp.zeros_like(qn_ref)
        kn_ref[...] = jnp.zeros_like(kn_ref)

    xb = xb_ref[...]
    qb = (jnp.dot(xb, wq_ref[...], preferred_element_type=F32) * Q_SCALE).astype(BF16)
    q_ref[...] = qb
    k = jnp.dot(xb, wk_ref[...], preferred_element_type=F32)
    kb = k.astype(BF16)
    kb_ref[...] = kb
    v = jnp.dot(xb, wv_ref[...], preferred_element_type=F32)
    vb_ref[...] = v.astype(BF16)
    tm = x_ref.shape[0]
    for hh in range(heads_per_step):
        rows = pl.ds(j * heads_per_step + hh, tm, stride=HEADS)
        k_ref[rows, :] = k[:, hh * HEAD_DIM:(hh + 1) * HEAD_DIM]
        v_ref[rows, :] = v[:, hh * HEAD_DIM:(hh + 1) * HEAD_DIM]
    qn_ref[...] = _max_sq_norm_per_head(qb, j * heads_per_step, qn_ref[...])
    kn_ref[...] = _max_sq_norm_per_head(kb, j * heads_per_step, kn_ref[...])


def _fox_proj(x, w_in, wf, bf, *, tm=512, tn=512):
    t, d = x.shape
    nd = d // tn
    row = pl.BlockSpec((tm, tn), lambda i, j: (i, j))

    def wcols(part):
        return pl.BlockSpec((d, tn), lambda i, j: (0, part * nd + j))

    wq, wk, wv = wcols(0), wcols(1), wcols(2)
    tile_stat = pl.BlockSpec((None, SUBLANES, LANES), lambda i, j: (i, 0, 0))
    by_head = pl.BlockSpec((tm * HEADS, HEAD_DIM), lambda i, j: (i, 0))
    return pl.pallas_call(
        _proj_kernel,
        grid=(t // tm, d // tn),
        in_specs=[
            pl.BlockSpec((tm, d), lambda i, j: (i, 0)),
            wq, wk, wv,
            pl.BlockSpec((d, LANES), lambda i, j: (0, 0)),
            pl.BlockSpec((1, LANES), lambda i, j: (0, 0)),
        ],
        out_specs=[row, by_head, by_head, row, row, pl.BlockSpec((tm, LANES), lambda i, j: (i, 0)), tile_stat, tile_stat],
        out_shape=[
            jax.ShapeDtypeStruct((t, d), BF16),
            jax.ShapeDtypeStruct((t * HEADS, HEAD_DIM), F32),
            jax.ShapeDtypeStruct((t * HEADS, HEAD_DIM), F32),
            jax.ShapeDtypeStruct((t, d), BF16),
            jax.ShapeDtypeStruct((t, d), BF16),
            jax.ShapeDtypeStruct((t, LANES), F32),
            jax.ShapeDtypeStruct((t // tm, SUBLANES, LANES), F32),
            jax.ShapeDtypeStruct((t // tm, SUBLANES, LANES), F32),
        ],
        scratch_shapes=[pltpu.VMEM((tm, d), BF16)],
        compiler_params=_params(("parallel", "arbitrary")),
        name="fox_proj",
    )(x, w_in, w_in, w_in, wf, bf)


def _split3(x):
    hi = x.astype(BF16)
    rem = x - hi.astype(F32)
    mid = rem.astype(BF16)
    lo = (rem - mid.astype(F32)).astype(BF16)
    return hi, mid, lo


def _cumsum_kernel(x_ref, o_ref, carry_ref, *, tc):
    @pl.when(pl.program_id(1) == 0)
    def _():
        carry_ref[...] = jnp.zeros_like(carry_ref)

    r = lax.broadcasted_iota(jnp.int32, (tc, tc), 0)
    c = lax.broadcasted_iota(jnp.int32, (tc, tc), 1)
    tri = (c <= r).astype(BF16)
    hi, mid, lo = _split3(x_ref[...])
    cs = (jnp.dot(tri, hi, preferred_element_type=F32)
          + jnp.dot(tri, mid, preferred_element_type=F32)
          + jnp.dot(tri, lo, preferred_element_type=F32)) + carry_ref[...]
    o_ref[...] = cs * LOG2E
    carry_ref[...] = cs[tc - 1:tc, :]


def _cumsum_time(x, *, tc):
    nb, t, w = x.shape
    return pl.pallas_call(
        functools.partial(_cumsum_kernel, tc=tc),
        grid=(nb, t // tc),
        in_specs=[pl.BlockSpec((None, tc, w), lambda b, j: (b, j, 0))],
        out_specs=pl.BlockSpec((None, tc, w), lambda b, j: (b, j, 0)),
        out_shape=jax.ShapeDtypeStruct((nb, t, w), F32),
        scratch_shapes=[pltpu.VMEM((1, w), F32)],
        compiler_params=_params(("parallel", "arbitrary")),
        name="cumsum_time",
    )(x)


N_SPLIT = 3
QK_AHEAD = 1


def _bias_selectors():
    sel_q = np.zeros((N_SPLIT * LANES, D_MODEL), np.float32)
    sel_k = np.zeros((N_SPLIT * LANES, D_MODEL), np.float32)
    one_q = np.zeros((1, D_MODEL), np.float32)
    one_k = np.zeros((1, D_MODEL), np.float32)
    for h in range(HEADS):
        for part in range(N_SPLIT):
            sel_q[part * LANES + h, h * HEAD_DIM + part] = 1.0
            sel_k[part * LANES + h, h * HEAD_DIM + N_SPLIT + part] = -1.0
            one_q[0, h * HEAD_DIM + N_SPLIT + part] = 1.0
            one_k[0, h * HEAD_DIM + part] = 1.0
    return jnp.asarray(sel_q, BF16), jnp.asarray(sel_k, BF16), jnp.asarray(one_q), jnp.asarray(one_k)


def _bias_operands_kernel(c_ref, sel_q_ref, sel_k_ref, one_q_ref, one_k_ref, a_ref, b_ref):
    parts = jnp.concatenate(_split3(c_ref[...]), axis=1)
    a_ref[...] = (jnp.dot(parts, sel_q_ref[...], preferred_element_type=F32) + one_q_ref[...]).astype(BF16)
    b_ref[...] = (jnp.dot(parts, sel_k_ref[...], preferred_element_type=F32) + one_k_ref[...]).astype(BF16)


def _bias_operands(c2, *, tr=1024):
    s, w = c2.shape
    rows = pl.BlockSpec((tr, D_MODEL), lambda i: (i, 0))
    sel = pl.BlockSpec((N_SPLIT * w, D_MODEL), lambda i: (0, 0))
    one = pl.BlockSpec((1, D_MODEL), lambda i: (0, 0))
    return pl.pallas_call(
        _bias_operands_kernel,
        grid=(s // tr,),
        in_specs=[pl.BlockSpec((tr, w), lambda i: (i, 0)), sel, sel, one, one],
        out_specs=[rows, rows],
        out_shape=[jax.ShapeDtypeStruct((s, D_MODEL), BF16), jax.ShapeDtypeStruct((s, D_MODEL), BF16)],
        compiler_params=_params(("parallel",)),
        name="bias_operands",
    )(c2, *_bias_selectors())


def _qk(q_h, k_h):
    return lax.dot_general(q_h, k_h, (((1,), (1,)), ((), ())), preferred_element_type=F32)


def _attn_schedule(qmax, kmax, c2, tq, nsteps):
    nq = qmax.shape[0]
    qn, kn = jnp.sqrt(qmax), jnp.sqrt(kmax)
    c_first, c_last = c2[0::tq, :HEADS], c2[tq - 1::tq, :HEADS]
    bound = qn[:, None, :] * (kn[None, :, :] + kn[:, None, :]) + c_first[:, None, :] - c_last[None, :, :]
    tile = jnp.arange(nq, dtype=jnp.int32)
    skip = jnp.all(bound < -SKIP_LOG2, axis=-1) & (tile[None, :] < tile[:, None])
    jstart = jnp.argmin(skip.astype(jnp.int32), axis=1).astype(jnp.int32)
    cnt = tile - jstart + 1
    ends = jnp.cumsum(cnt)
    starts = ends - cnt
    n = jnp.arange(nsteps, dtype=jnp.int32)
    valid = n < ends[-1]
    qi = jnp.minimum(jnp.sum((ends[None, :] <= n[:, None]).astype(jnp.int32), axis=1), nq - 1)
    kj = jstart[qi] + n - starts[qi]
    qi = jnp.where(valid, qi, nq - 1)
    kj = jnp.where(valid, kj, nq - 1)
    first = valid & (kj == jstart[qi])
    return qi, kj, valid.astype(jnp.int32) + 2 * first.astype(jnp.int32)


def _attn_prompt_kernel(qi_ref, kj_ref, fl_ref, q_ref, qc_ref, k_ref, kc_ref, v_ref, o_ref,
                        m_ref, l_ref, acc_ref, *, tq):
    n = pl.program_id(0)
    flags = fl_ref[n]
    nchunk = tq // LANES

    @pl.when(flags >= 2)
    def _():
        m_ref[...] = jnp.full_like(m_ref, NEG_BIG)
        l_ref[...] = jnp.zeros_like(l_ref)
        acc_ref[...] = jnp.zeros_like(acc_ref)

    def sweep(masked):
        if masked:
            row = lax.broadcasted_iota(jnp.int32, (tq, LANES), 0)
            lane = lax.broadcasted_iota(jnp.int32, (tq, LANES), 1)

        def logits(h):
            sl = slice(h * HEAD_DIM, (h + 1) * HEAD_DIM)
            return _qk(jnp.concatenate([q_ref[:, sl], qc_ref[:, sl]], axis=1),
                       jnp.concatenate([k_ref[:, sl], kc_ref[:, sl]], axis=1))

        pending = [logits(h) for h in range(QK_AHEAD)]
        for h in range(HEADS):
            sl = slice(h * HEAD_DIM, (h + 1) * HEAD_DIM)
            if h + QK_AHEAD < HEADS:
                pending.append(logits(h + QK_AHEAD))
            s = pending.pop(0)
            chunks = []
            for c in range(nchunk):
                sc = s[:, c * LANES:(c + 1) * LANES]
                if masked:
                    sc = jnp.where(lane + c * LANES <= row, sc, NEG_BIG)
                chunks.append(sc)
            mx = chunks[0]
            for sc in chunks[1:]:
                mx = jnp.maximum(mx, sc)
            m_prev = m_ref[h]
            m_new = jnp.maximum(m_prev, jnp.max(mx, axis=-1, keepdims=True))
            a = jnp.exp2(m_prev - m_new)
            ps = [jnp.exp2(sc - m_new) for sc in chunks]
            lsum = ps[0]
            for p in ps[1:]:
                lsum = lsum + p
            l_ref[h] = a * l_ref[h] + lsum
            p = jnp.concatenate(ps, axis=-1).astype(BF16)
            acc_ref[:, sl] = a * acc_ref[:, sl] + jnp.dot(p, v_ref[:, sl], preferred_element_type=F32)
            m_ref[h] = m_new

    diagonal = kj_ref[n] == qi_ref[n]

    @pl.when((flags >= 1) & jnp.logical_not(diagonal))
    def _():
        sweep(False)

    @pl.when((flags >= 1) & diagonal)
    def _():
        sweep(True)
        for h in range(HEADS):
            sl = slice(h * HEAD_DIM, (h + 1) * HEAD_DIM)
            l_tot = jnp.sum(l_ref[h], axis=-1, keepdims=True)
            o_ref[:, sl] = (acc_ref[:, sl] / l_tot).astype(BF16)


def _attn_prompt(q, kb, vb, c2, qmax, kmax, *, tq=512):
    s, d = q.shape
    nq = s // tq
    nsteps = nq * (nq + 1) // 2
    qi, kj, flags = _attn_schedule(qmax, kmax, c2, tq, nsteps)
    qc, kc = _bias_operands(c2)
    qrow = pl.BlockSpec((tq, d), lambda n, qi, kj, fl: (qi[n], 0))
    krow = pl.BlockSpec((tq, d), lambda n, qi, kj, fl: (kj[n], 0))
    stat = pltpu.VMEM((HEADS, tq, LANES), F32)
    return pl.pallas_call(
        functools.partial(_attn_prompt_kernel, tq=tq),
        grid_spec=pltpu.PrefetchScalarGridSpec(
            num_scalar_prefetch=3,
            grid=(nsteps,),
            in_specs=[qrow, qrow, krow, krow, krow],
            out_specs=qrow,
            scratch_shapes=[stat, stat, pltpu.VMEM((tq, d), F32)],
        ),
        out_shape=jax.ShapeDtypeStruct((s, d), BF16),
        compiler_params=_params(("arbitrary",)),
        name="attn_prompt",
    )(qi, kj, flags, q, qc, kb, kc, vb)


def _attn_sample_kernel(q_ref, kc_ref, vc_ref, kn_ref, vn_ref, cq_ref, ctc_ref, ctn_ref, o_ref,
                        m_ref, l_ref, acc_ref, cqr_ref, *, tp):
    j = pl.program_id(1)
    tnew = q_ref.shape[0]
    heads = [slice(h * HEAD_DIM, (h + 1) * HEAD_DIM) for h in range(HEADS)]

    @pl.when(j == 0)
    def _():
        m_ref[...] = jnp.full_like(m_ref, NEG_BIG)
        l_ref[...] = jnp.zeros_like(l_ref)
        acc_ref[...] = jnp.zeros_like(acc_ref)
        for h in range(HEADS):
            cqr_ref[h * tnew:(h + 1) * tnew, :] = jnp.broadcast_to(cq_ref[:, h:h + 1], (tnew, LANES))

    def update(keys, values, key_c2, visible):
        n = key_c2.shape[1]
        s = jnp.concatenate([_qk(q_ref[:, heads[h]], keys[h]) for h in range(HEADS)], axis=0)
        c_keys = jnp.concatenate([jnp.broadcast_to(key_c2[h:h + 1, :], (tnew, n)) for h in range(HEADS)], axis=0)
        s = s + cqr_ref[:, :1] - c_keys
        if visible is not None:
            s = jnp.where(visible, s, NEG_BIG)
        m_prev = m_ref[...]
        m_new = jnp.maximum(m_prev, jnp.max(s, axis=-1, keepdims=True))
        a = jnp.exp2(m_prev - m_new)
        p = jnp.exp2(s - m_new[:, :1])
        l_ref[...] = a * l_ref[...] + jnp.sum(p, axis=-1, keepdims=True)
        pb = p.astype(BF16)
        pv = jnp.concatenate([jnp.dot(pb[h * tnew:(h + 1) * tnew], values[h], preferred_element_type=F32)
                              for h in range(HEADS)], axis=0)
        acc_ref[...] = a * acc_ref[...] + pv
        m_ref[...] = m_new

    update([kc_ref[pl.ds(h, tp, stride=HEADS), :].astype(BF16) for h in range(HEADS)],
           [vc_ref[pl.ds(h, tp, stride=HEADS), :].astype(BF16) for h in range(HEADS)],
           ctc_ref[...], None)

    @pl.when(j == pl.num_programs(1) - 1)
    def _():
        query = lax.broadcasted_iota(jnp.int32, (HEADS * tnew, tnew), 0) % tnew
        key = lax.broadcasted_iota(jnp.int32, (HEADS * tnew, tnew), 1)
        update([kn_ref[:, sl] for sl in heads], [vn_ref[:, sl] for sl in heads], ctn_ref[...], key <= query)
        out = acc_ref[...] / l_ref[...]
        for h in range(HEADS):
            o_ref[:, heads[h]] = out[h * tnew:(h + 1) * tnew, :].astype(BF16)


def _attn_sample(q, k_cache, v_cache, kb_new, vb_new, c_new, ct_cache, ct_new, *, tnew, tp=512):
    t, d = q.shape
    nb, past_rows, _ = k_cache.shape
    new_rows = pl.BlockSpec((tnew, d), lambda b, j: (b, 0))
    cache = pl.BlockSpec((None, tp * HEADS, HEAD_DIM), lambda b, j: (b, j, 0))
    return pl.pallas_call(
        functools.partial(_attn_sample_kernel, tp=tp),
        grid=(nb, past_rows // (tp * HEADS)),
        in_specs=[
            new_rows, cache, cache, new_rows, new_rows,
            pl.BlockSpec((tnew, LANES), lambda b, j: (b, 0)),
            pl.BlockSpec((None, HEADS, tp), lambda b, j: (b, 0, j)),
            pl.BlockSpec((None, HEADS, tnew), lambda b, j: (b, 0, 0)),
        ],
        out_specs=new_rows,
        out_shape=jax.ShapeDtypeStruct((t, d), BF16),
        scratch_shapes=[pltpu.VMEM((HEADS * tnew, LANES), F32), pltpu.VMEM((HEADS * tnew, LANES), F32),
                        pltpu.VMEM((HEADS * tnew, HEAD_DIM), F32), pltpu.VMEM((HEADS * tnew, LANES), F32)],
        compiler_params=_params(("parallel", "arbitrary")),
        name="attn_sample",
    )(q, k_cache, v_cache, kb_new, vb_new, c_new, ct_cache, ct_new)


def _oproj_kernel(x_ref, o_ref, wo_ref, g_ref, b_ref, out_ref):
    y = DN_ALPHA * x_ref[...] + jnp.dot(o_ref[...], wo_ref[...], preferred_element_type=F32)
    out_ref[...] = _layer_norm(y, g_ref[...], b_ref[...])


def _oproj_ln(x, o, wo, g, b, *, tm=512):
    t, d = x.shape
    row = pl.BlockSpec((tm, d), lambda i: (i, 0))
    vec = pl.BlockSpec((1, d), lambda i: (0, 0))
    return pl.pallas_call(
        _oproj_kernel,
        grid=(t // tm,),
        in_specs=[row, row, pl.BlockSpec((d, d), lambda i: (0, 0)), vec, vec],
        out_specs=row,
        out_shape=jax.ShapeDtypeStruct((t, d), F32),
        compiler_params=_params(("parallel",)),
        name="oproj_ln",
    )(x, o, wo, g, b)


def kernel(x_prompt, x_sample, state_pool, cache_fox_k, cache_fox_v, cache_fox_logf, ln_g, ln_b,
           ffn_w1, ffn_w3, ffn_w2, pool_w, pool_scale, fox_w_in, fox_b_f, fox_w_o):
    d = D_MODEL
    _, seq, _ = x_prompt.shape
    nb, tnew, _ = x_sample.shape
    past = cache_fox_k.shape[2]
    ns = nb * tnew
    w1b, w3b, w2b = ffn_w1.astype(BF16), ffn_w3.astype(BF16), ffn_w2.astype(BF16)

    def ffn(xp, xs, i, s, ln_idx):
        args = (w1b, w3b, w2b, i, s, ln_g[i, ln_idx][None], ln_b[i, ln_idx][None])
        return _ffn_ln(xp, *args, tm=512, tf=512), _ffn_ln(xs, *args, tm=ns, tf=1024)

    xp, xs = x_prompt.reshape(seq, d), x_sample.reshape(ns, d)

    xp, xs = ffn(xp, xs, 0, 0, 0)
    xp3, xs3 = xp.reshape(1, seq, d), xs.reshape(nb, tnew, d)
    pool_prompt = xp3[:, seq - POOL_STATE:][None]
    pool_sample = jnp.concatenate([state_pool[0], xs3], axis=1)[:, -POOL_STATE:][None]
    pw = pool_w[0].astype(BF16)
    ps, g1, b1 = pool_scale[0][None], ln_g[0, 1][None], ln_b[0, 1][None]
    tm = 512
    xp = _pool_ln(xp3, xp3, lambda bi, i: (bi, jnp.maximum(i * (tm // HALO) - 1, 0), 0), pw, ps, g1, b1,
                  bb=1, tm=tm, start_pos=0, zero_first=True).reshape(seq, d)
    prev = jnp.pad(state_pool[0], ((0, 0), (HALO - POOL_STATE, 0), (0, 0)))
    xs = _pool_ln(xs3, prev, lambda bi, i: (bi, 0, 0), pw, ps, g1, b1,
                  bb=nb, tm=tnew, start_pos=past, zero_first=False).reshape(ns, d)
    xp, xs = ffn(xp, xs, 0, 1, 2)

    xp, xs = ffn(xp, xs, 1, 0, 0)
    w_in = fox_w_in[0].astype(BF16)
    wf = jnp.pad(w_in[:, 3 * d:], ((0, 0), (0, LANES - HEADS)))
    bf = jnp.pad(fox_b_f[0], (0, LANES - HEADS))[None]
    q_p, k_p, v_p, kb_p, vb_p, lf_p, qmax, kmax = _fox_proj(xp, w_in, wf, bf)
    q_s, k_s, v_s, kb_s, vb_s, lf_s, _, _ = _fox_proj(xs, w_in, wf, bf)

    c_p = _cumsum_time(lf_p[None], tc=512)[0]
    o_p = _attn_prompt(q_p, kb_p, vb_p, c_p, qmax[:, 0, :HEADS], kmax[:, 0, :HEADS])

    lf_cache = jnp.pad(cache_fox_logf[0], ((0, 0), (0, 0), (0, LANES - HEADS)))
    lf_all = jnp.concatenate([lf_cache, lf_s.reshape(nb, tnew, LANES)], axis=1)
    c_s = _cumsum_time(lf_all, tc=(past + tnew) // 3)
    ct_s = c_s[:, :, :HEADS].transpose(0, 2, 1)
    o_s = _attn_sample(q_s, cache_fox_k[0].reshape(nb, past * HEADS, HEAD_DIM),
                       cache_fox_v[0].reshape(nb, past * HEADS, HEAD_DIM), kb_s, vb_s,
                       c_s[:, past:].reshape(ns, LANES), ct_s[:, :, :past], ct_s[:, :, past:], tnew=tnew)

    wo = fox_w_o[0].astype(BF16)
    g1, b1 = ln_g[1, 1][None], ln_b[1, 1][None]
    xp, xs = _oproj_ln(xp, o_p, wo, g1, b1), _oproj_ln(xs, o_s, wo, g1, b1)
    xp, xs = ffn(xp, xs, 1, 1, 2)

    shp = (HEADS, HEAD_DIM)
    return (xp.reshape(1, seq, d), xs.reshape(nb, tnew, d), pool_prompt, pool_sample,
            k_p.reshape(1, 1, seq, *shp), v_p.reshape(1, 1, seq, *shp), lf_p[:, :HEADS].reshape(1, 1, seq, HEADS),
            k_s.reshape(1, nb, tnew, *shp), v_s.reshape(1, nb, tnew, *shp),
            lf_s[:, :HEADS].reshape(1, nb, tnew, HEADS))
```

```python
import functools
import math

import jax
import jax.numpy as jnp
import numpy as np
from jax import lax
from jax.experimental import pallas as pl
from jax.experimental.pallas import tpu as pltpu

F32 = jnp.float32
BF16 = jnp.bfloat16

D_MODEL = 2048
DEPTH = 2
POOL_WINDOWS = (2, 4, 8, 16)
POOL_GROUP = D_MODEL // len(POOL_WINDOWS)
POOL_STATE = max(POOL_WINDOWS) - 1
HALO = POOL_STATE + 1
HEAD_DIM = 128
HEADS = D_MODEL // HEAD_DIM
LN_EPS = 1e-5
DN_ALPHA = (2 * DEPTH) ** 0.25
LOG2E = math.log2(math.e)
Q_SCALE = HEAD_DIM ** -0.5 * LOG2E
NEG_BIG = -1e30
SKIP_LOG2 = 150.0
LANES = 128
SUBLANES = 8
VMEM_LIMIT = 60 * 1024 * 1024


def _params(semantics):
    return pltpu.CompilerParams(dimension_semantics=semantics, vmem_limit_bytes=VMEM_LIMIT)


def _layer_norm(y, g, b):
    mu = jnp.mean(y, axis=-1, keepdims=True)
    yc = y - mu
    var = jnp.mean(yc * yc, axis=-1, keepdims=True)
    return yc * lax.rsqrt(var + LN_EPS) * g + b


def _ffn_kernel(x_ref, w1_ref, w3_ref, w2_ref, g_ref, b_ref, o_ref, *rest):
    *wb_refs, xb_ref = rest
    j = pl.program_id(1)

    @pl.when(j == 0)
    def _():
        xb_ref[...] = x_ref[...].astype(BF16)
        o_ref[...] = jnp.zeros_like(o_ref)

    w1, w3, w2 = w1_ref[...].astype(BF16), w3_ref[...].astype(BF16), w2_ref[...].astype(BF16)
    for wb_ref, w in zip(wb_refs, (w1, w3, w2)):
        wb_ref[...] = w
    xb = xb_ref[...]
    h1 = jnp.dot(xb, w1, preferred_element_type=F32)
    h3 = jnp.dot(xb, w3, preferred_element_type=F32)
    gate = (h1 * jax.nn.sigmoid(h1) * h3).astype(BF16)
    o_ref[...] += jnp.dot(gate, w2, preferred_element_type=F32)

    @pl.when(j == pl.num_programs(1) - 1)
    def _():
        y = DN_ALPHA * x_ref[...] + 0.5 * o_ref[...]
        o_ref[...] = _layer_norm(y, g_ref[...], b_ref[...])


def _ffn_ln(x, w1, w3, w2, g, b, *, tm, tf, layer_half=None):
    t, d = x.shape
    f = w1.shape[-1]
    emit = layer_half is not None
    assert not emit or t == tm

    def wspec(block, idx):
        if emit:
            return pl.BlockSpec((None, None) + block, lambda i, j: layer_half + idx(j))
        return pl.BlockSpec(block, lambda i, j: idx(j))

    up, down = wspec((d, tf), lambda j: (0, j)), wspec((tf, d), lambda j: (j, 0))
    out_specs = [pl.BlockSpec((tm, d), lambda i, j: (i, 0))]
    out_shape = [jax.ShapeDtypeStruct((t, d), F32)]
    if emit:
        out_specs += [pl.BlockSpec((d, tf), lambda i, j: (0, j)), pl.BlockSpec((d, tf), lambda i, j: (0, j)),
                      pl.BlockSpec((tf, d), lambda i, j: (j, 0))]
        out_shape += [jax.ShapeDtypeStruct((d, f), BF16), jax.ShapeDtypeStruct((d, f), BF16),
                      jax.ShapeDtypeStruct((f, d), BF16)]
    res = pl.pallas_call(
        _ffn_kernel,
        grid=(t // tm, f // tf),
        in_specs=[
            pl.BlockSpec((tm, d), lambda i, j: (i, 0)),
            up, up, down,
            pl.BlockSpec((1, d), lambda i, j: (0, 0)),
            pl.BlockSpec((1, d), lambda i, j: (0, 0)),
        ],
        out_specs=out_specs,
        out_shape=out_shape,
        scratch_shapes=[pltpu.VMEM((tm, d), BF16)],
        compiler_params=_params(("parallel", "arbitrary")),
        name="ffn_ln",
    )(x, w1, w3, w2, g, b)
    return res if emit else res[0]


def _pool_kernel(x_ref, halo_ref, pw_ref, ps_ref, g_ref, b_ref, o_ref, buf_ref, *, tm, start_pos, zero_first):
    i = pl.program_id(1)
    bb = x_ref.shape[0]
    x = x_ref[...]
    halo = halo_ref[...]
    if zero_first:
        halo = jnp.where(i == 0, 0.0, halo)
    pos = start_pos + i * tm + lax.broadcasted_iota(jnp.int32, (1, tm, 1), 1)
    ys = []
    for g, w in enumerate(POOL_WINDOWS):
        cols = slice(g * POOL_GROUP, (g + 1) * POOL_GROUP)
        xg = x[:, :, cols]
        buf_ref[:, 0:HALO, :] = halo[:, :, cols]
        buf_ref[:, HALO:, :] = xg
        win = xg
        for k in range(1, w):
            win = win + buf_ref[:, HALO - k:HALO - k + tm, :]
        cnt = jnp.minimum(pos + 1, w).astype(F32)
        diff = (win / cnt - xg).reshape(bb * tm, POOL_GROUP)
        ys.append(jnp.dot(diff.astype(BF16), pw_ref[g], preferred_element_type=F32))
    y = jnp.concatenate(ys, axis=-1) * ps_ref[...]
    out = _layer_norm(DN_ALPHA * x.reshape(bb * tm, D_MODEL) + y, g_ref[...], b_ref[...])
    o_ref[...] = out.reshape(bb, tm, D_MODEL)


def _pool_ln(x, halo_src, halo_map, pw, ps, g, b, *, bb, tm, start_pos, zero_first):
    nb, t, d = x.shape
    kern = functools.partial(_pool_kernel, tm=tm, start_pos=start_pos, zero_first=zero_first)
    return pl.pallas_call(
        kern,
        grid=(nb // bb, t // tm),
        in_specs=[
            pl.BlockSpec((bb, tm, d), lambda bi, i: (bi, i, 0)),
            pl.BlockSpec((bb, HALO, d), halo_map),
            pl.BlockSpec((len(POOL_WINDOWS), POOL_GROUP, POOL_GROUP), lambda bi, i: (0, 0, 0)),
            pl.BlockSpec((1, d), lambda bi, i: (0, 0)),
            pl.BlockSpec((1, d), lambda bi, i: (0, 0)),
            pl.BlockSpec((1, d), lambda bi, i: (0, 0)),
        ],
        out_specs=pl.BlockSpec((bb, tm, d), lambda bi, i: (bi, i, 0)),
        out_shape=jax.ShapeDtypeStruct((nb, t, d), F32),
        scratch_shapes=[pltpu.VMEM((bb, HALO + tm, POOL_GROUP), F32)],
        compiler_params=_params(("parallel", "arbitrary")),
        name="pool_ln",
    )(x, halo_src, pw, ps, g, b)


def _max_sq_norm_per_head(xb, first_head, acc):
    sq = xb.astype(F32)
    sq = sq * sq
    lane = lax.broadcasted_iota(jnp.int32, acc.shape, 1)
    for hh in range(xb.shape[1] // HEAD_DIM):
        row = jnp.sum(sq[:, hh * HEAD_DIM:(hh + 1) * HEAD_DIM], axis=-1, keepdims=True)
        acc = jnp.where(lane == first_head + hh, jnp.max(row, axis=0, keepdims=True), acc)
    return acc


def _proj_kernel(x_ref, wq_ref, wk_ref, wv_ref, wf_ref, bf_ref,
                 q_ref, k_ref, v_ref, kb_ref, vb_ref, lf_ref, qn_ref, kn_ref, xb_ref):
    j = pl.program_id(1)
    heads_per_step = q_ref.shape[1] // HEAD_DIM

    @pl.when(j == 0)
    def _():
        xb = x_ref[...].astype(BF16)
        xb_ref[...] = xb
        fl = jnp.dot(xb, wf_ref[...], preferred_element_type=F32) + bf_ref[...]
        lf_ref[...] = jnp.minimum(fl, 0.0) - jnp.log1p(jnp.exp(-jnp.abs(fl)))
        qn_ref[...] = jnp.zeros_like(qn_ref)
        kn_ref[...] = jnp.zeros_like(kn_ref)

    xb = xb_ref[...]
    qb = (jnp.dot(xb, wq_ref[...], preferred_element_type=F32) * Q_SCALE).astype(BF16)
    q_ref[...] = qb
    k = jnp.dot(xb, wk_ref[...], preferred_element_type=F32)
    kb = k.astype(BF16)
    kb_ref[...] = kb
    v = jnp.dot(xb, wv_ref[...], preferred_element_type=F32)
    vb_ref[...] = v.astype(BF16)
    tm = x_ref.shape[0]
    for hh in range(heads_per_step):
        rows = pl.ds(j * heads_per_step + hh, tm, stride=HEADS)
        k_ref[rows, :] = k[:, hh * HEAD_DIM:(hh + 1) * HEAD_DIM]
        v_ref[rows, :] = v[:, hh * HEAD_DIM:(hh + 1) * HEAD_DIM]
    qn_ref[...] = _max_sq_norm_per_head(qb, j * heads_per_step, qn_ref[...])
    kn_ref[...] = _max_sq_norm_per_head(kb, j * heads_per_step, kn_ref[...])


def _fox_proj(x, w_in, wf, bf, *, tm=512, tn=512):
    t, d = x.shape
    nd = d // tn
    row = pl.BlockSpec((tm, tn), lambda i, j: (i, j))

    def wcols(part):
        return pl.BlockSpec((d, tn), lambda i, j: (0, part * nd + j))

    wq, wk, wv = wcols(0), wcols(1), wcols(2)
    tile_stat = pl.BlockSpec((None, SUBLANES, LANES), lambda i, j: (i, 0, 0))
    by_head = pl.BlockSpec((tm * HEADS, HEAD_DIM), lambda i, j: (i, 0))
    return pl.pallas_call(
        _proj_kernel,
        grid=(t // tm, d // tn),
        in_specs=[
            pl.BlockSpec((tm, d), lambda i, j: (i, 0)),
            wq, wk, wv,
            pl.BlockSpec((d, LANES), lambda i, j: (0, 0)),
            pl.BlockSpec((1, LANES), lambda i, j: (0, 0)),
        ],
        out_specs=[row, by_head, by_head, row, row, pl.BlockSpec((tm, LANES), lambda i, j: (i, 0)), tile_stat, tile_stat],
        out_shape=[
            jax.ShapeDtypeStruct((t, d), BF16),
            jax.ShapeDtypeStruct((t * HEADS, HEAD_DIM), F32),
            jax.ShapeDtypeStruct((t * HEADS, HEAD_DIM), F32),
            jax.ShapeDtypeStruct((t, d), BF16),
            jax.ShapeDtypeStruct((t, d), BF16),
            jax.ShapeDtypeStruct((t, LANES), F32),
            jax.ShapeDtypeStruct((t // tm, SUBLANES, LANES), F32),
            jax.ShapeDtypeStruct((t // tm, SUBLANES, LANES), F32),
        ],
        scratch_shapes=[pltpu.VMEM((tm, d), BF16)],
        compiler_params=_params(("parallel", "arbitrary")),
        name="fox_proj",
    )(x, w_in, w_in, w_in, wf, bf)


def _split3(x):
    hi = x.astype(BF16)
    rem = x - hi.astype(F32)
    mid = rem.astype(BF16)
    lo = (rem - mid.astype(F32)).astype(BF16)
    return hi, mid, lo


def _cumsum_kernel(x_ref, o_ref, carry_ref, *, tc):
    @pl.when(pl.program_id(1) == 0)
    def _():
        carry_ref[...] = jnp.zeros_like(carry_ref)

    r = lax.broadcasted_iota(jnp.int32, (tc, tc), 0)
    c = lax.broadcasted_iota(jnp.int32, (tc, tc), 1)
    tri = (c <= r).astype(BF16)
    hi, mid, lo = _split3(x_ref[...])
    cs = (jnp.dot(tri, hi, preferred_element_type=F32)
          + jnp.dot(tri, mid, preferred_element_type=F32)
          + jnp.dot(tri, lo, preferred_element_type=F32)) + carry_ref[...]
    o_ref[...] = cs * LOG2E
    carry_ref[...] = cs[tc - 1:tc, :]


def _cumsum_time(x, *, tc):
    nb, t, w = x.shape
    return pl.pallas_call(
        functools.partial(_cumsum_kernel, tc=tc),
        grid=(nb, t // tc),
        in_specs=[pl.BlockSpec((None, tc, w), lambda b, j: (b, j, 0))],
        out_specs=pl.BlockSpec((None, tc, w), lambda b, j: (b, j, 0)),
        out_shape=jax.ShapeDtypeStruct((nb, t, w), F32),
        scratch_shapes=[pltpu.VMEM((1, w), F32)],
        compiler_params=_params(("parallel", "arbitrary")),
        name="cumsum_time",
    )(x)


N_SPLIT = 3
QK_AHEAD = 1


def _bias_selectors():
    sel_q = np.zeros((N_SPLIT * LANES, D_MODEL), np.float32)
    sel_k = np.zeros((N_SPLIT * LANES, D_MODEL), np.float32)
    one_q = np.zeros((1, D_MODEL), np.float32)
    one_k = np.zeros((1, D_MODEL), np.float32)
    for h in range(HEADS):
        for part in range(N_SPLIT):
            sel_q[part * LANES + h, h * HEAD_DIM + part] = 1.0
            sel_k[part * LANES + h, h * HEAD_DIM + N_SPLIT + part] = -1.0
            one_q[0, h * HEAD_DIM + N_SPLIT + part] = 1.0
            one_k[0, h * HEAD_DIM + part] = 1.0
    return jnp.asarray(sel_q, BF16), jnp.asarray(sel_k, BF16), jnp.asarray(one_q), jnp.asarray(one_k)


def _bias_operands_kernel(c_ref, sel_q_ref, sel_k_ref, one_q_ref, one_k_ref, a_ref, b_ref):
    parts = jnp.concatenate(_split3(c_ref[...]), axis=1)
    a_ref[...] = (jnp.dot(parts, sel_q_ref[...], preferred_element_type=F32) + one_q_ref[...]).astype(BF16)
    b_ref[...] = (jnp.dot(parts, sel_k_ref[...], preferred_element_type=F32) + one_k_ref[...]).astype(BF16)


def _bias_operands(c2, *, tr=1024):
    s, w = c2.shape
    rows = pl.BlockSpec((tr, D_MODEL), lambda i: (i, 0))
    sel = pl.BlockSpec((N_SPLIT * w, D_MODEL), lambda i: (0, 0))
    one = pl.BlockSpec((1, D_MODEL), lambda i: (0, 0))
    return pl.pallas_call(
        _bias_operands_kernel,
        grid=(s // tr,),
        in_specs=[pl.BlockSpec((tr, w), lambda i: (i, 0)), sel, sel, one, one],
        out_specs=[rows, rows],
        out_shape=[jax.ShapeDtypeStruct((s, D_MODEL), BF16), jax.ShapeDtypeStruct((s, D_MODEL), BF16)],
        compiler_params=_params(("parallel",)),
        name="bias_operands",
    )(c2, *_bias_selectors())


def _qk(q_h, k_h):
    return lax.dot_general(q_h, k_h, (((1,), (1,)), ((), ())), preferred_element_type=F32)


def _attn_schedule(qmax, kmax, c2, tq, nsteps):
    nq = qmax.shape[0]
    qn, kn = jnp.sqrt(qmax), jnp.sqrt(kmax)
    c_first, c_last = c2[0::tq, :HEADS], c2[tq - 1::tq, :HEADS]
    bound = qn[:, None, :] * (kn[None, :, :] + kn[:, None, :]) + c_first[:, None, :] - c_last[None, :, :]
    tile = jnp.arange(nq, dtype=jnp.int32)
    skip = jnp.all(bound < -SKIP_LOG2, axis=-1) & (tile[None, :] < tile[:, None])
    jstart = jnp.argmin(skip.astype(jnp.int32), axis=1).astype(jnp.int32)
    cnt = tile - jstart + 1
    ends = jnp.cumsum(cnt)
    starts = ends - cnt
    n = jnp.arange(nsteps, dtype=jnp.int32)
    valid = n < ends[-1]
    qi = jnp.minimum(jnp.sum((ends[None, :] <= n[:, None]).astype(jnp.int32), axis=1), nq - 1)
    kj = jstart[qi] + n - starts[qi]
    qi = jnp.where(valid, qi, nq - 1)
    kj = jnp.where(valid, kj, nq - 1)
    first = valid & (kj == jstart[qi])
    return qi, kj, valid.astype(jnp.int32) + 2 * first.astype(jnp.int32)


def _attn_prompt_kernel(qi_ref, kj_ref, fl_ref, q_ref, qc_ref, k_ref, kc_ref, v_ref, o_ref,
                        m_ref, l_ref, acc_ref, *, tq):
    n = pl.program_id(0)
    flags = fl_ref[n]
    nchunk = tq // LANES

    @pl.when(flags >= 2)
    def _():
        m_ref[...] = jnp.full_like(m_ref, NEG_BIG)
        l_ref[...] = jnp.zeros_like(l_ref)
        acc_ref[...] = jnp.zeros_like(acc_ref)

    def sweep(masked):
        if masked:
            row = lax.broadcasted_iota(jnp.int32, (tq, LANES), 0)
            lane = lax.broadcasted_iota(jnp.int32, (tq, LANES), 1)

        def logits(h):
            sl = slice(h * HEAD_DIM, (h + 1) * HEAD_DIM)
            return _qk(jnp.concatenate([q_ref[:, sl], qc_ref[:, sl]], axis=1),
                       jnp.concatenate([k_ref[:, sl], kc_ref[:, sl]], axis=1))

        pending = [logits(h) for h in range(QK_AHEAD)]
        for h in range(HEADS):
            sl = slice(h * HEAD_DIM, (h + 1) * HEAD_DIM)
            if h + QK_AHEAD < HEADS:
                pending.append(logits(h + QK_AHEAD))
            s = pending.pop(0)
            chunks = []
            for c in range(nchunk):
                sc = s[:, c * LANES:(c + 1) * LANES]
                if masked:
                    sc = jnp.where(lane + c * LANES <= row, sc, NEG_BIG)
                chunks.append(sc)
            mx = chunks[0]
            for sc in chunks[1:]:
                mx = jnp.maximum(mx, sc)
            m_prev = m_ref[h]
            m_new = jnp.maximum(m_prev, jnp.max(mx, axis=-1, keepdims=True))
            a = jnp.exp2(m_prev - m_new)
            ps = [jnp.exp2(sc - m_new) for sc in chunks]
            lsum = ps[0]
            for p in ps[1:]:
                lsum = lsum + p
            l_ref[h] = a * l_ref[h] + lsum
            p = jnp.concatenate(ps, axis=-1).astype(BF16)
            acc_ref[:, sl] = a * acc_ref[:, sl] + jnp.dot(p, v_ref[:, sl], preferred_element_type=F32)
            m_ref[h] = m_new

    diagonal = kj_ref[n] == qi_ref[n]

    @pl.when((flags >= 1) & jnp.logical_not(diagonal))
    def _():
        sweep(False)

    @pl.when((flags >= 1) & diagonal)
    def _():
        sweep(True)
        for h in range(HEADS):
            sl = slice(h * HEAD_DIM, (h + 1) * HEAD_DIM)
            l_tot = jnp.sum(l_ref[h], axis=-1, keepdims=True)
            o_ref[:, sl] = (acc_ref[:, sl] / l_tot).astype(BF16)


def _attn_prompt(q, kb, vb, c2, qmax, kmax, *, tq=512):
    s, d = q.shape
    nq = s // tq
    nsteps = nq * (nq + 1) // 2
    qi, kj, flags = _attn_schedule(qmax, kmax, c2, tq, nsteps)
    qc, kc = _bias_operands(c2)
    qrow = pl.BlockSpec((tq, d), lambda n, qi, kj, fl: (qi[n], 0))
    krow = pl.BlockSpec((tq, d), lambda n, qi, kj, fl: (kj[n], 0))
    stat = pltpu.VMEM((HEADS, tq, LANES), F32)
    return pl.pallas_call(
        functools.partial(_attn_prompt_kernel, tq=tq),
        grid_spec=pltpu.PrefetchScalarGridSpec(
            num_scalar_prefetch=3,
            grid=(nsteps,),
            in_specs=[qrow, qrow, krow, krow, krow],
            out_specs=qrow,
            scratch_shapes=[stat, stat, pltpu.VMEM((tq, d), F32)],
        ),
        out_shape=jax.ShapeDtypeStruct((s, d), BF16),
        compiler_params=_params(("arbitrary",)),
        name="attn_prompt",
    )(qi, kj, flags, q, qc, kb, kc, vb)


def _attn_sample_kernel(q_ref, kc_ref, vc_ref, kn_ref, vn_ref, cq_ref, ctc_ref, ctn_ref, o_ref,
                        m_ref, l_ref, acc_ref, cqr_ref, *, tp):
    j = pl.program_id(1)
    tnew = q_ref.shape[0]
    heads = [slice(h * HEAD_DIM, (h + 1) * HEAD_DIM) for h in range(HEADS)]

    @pl.when(j == 0)
    def _():
        m_ref[...] = jnp.full_like(m_ref, NEG_BIG)
        l_ref[...] = jnp.zeros_like(l_ref)
        acc_ref[...] = jnp.zeros_like(acc_ref)
        for h in range(HEADS):
            cqr_ref[h * tnew:(h + 1) * tnew, :] = jnp.broadcast_to(cq_ref[:, h:h + 1], (tnew, LANES))

    def update(keys, values, key_c2, visible):
        n = key_c2.shape[1]
        s = jnp.concatenate([_qk(q_ref[:, heads[h]], keys[h]) for h in range(HEADS)], axis=0)
        c_keys = jnp.concatenate([jnp.broadcast_to(key_c2[h:h + 1, :], (tnew, n)) for h in range(HEADS)], axis=0)
        s = s + cqr_ref[:, :1] - c_keys
        if visible is not None:
            s = jnp.where(visible, s, NEG_BIG)
        m_prev = m_ref[...]
        m_new = jnp.maximum(m_prev, jnp.max(s, axis=-1, keepdims=True))
        a = jnp.exp2(m_prev - m_new)
        p = jnp.exp2(s - m_new[:, :1])
        l_ref[...] = a * l_ref[...] + jnp.sum(p, axis=-1, keepdims=True)
        pb = p.astype(BF16)
        pv = jnp.concatenate([jnp.dot(pb[h * tnew:(h + 1) * tnew], values[h], preferred_element_type=F32)
                              for h in range(HEADS)], axis=0)
        acc_ref[...] = a * acc_ref[...] + pv
        m_ref[...] = m_new

    update([kc_ref[pl.ds(h, tp, stride=HEADS), :].astype(BF16) for h in range(HEADS)],
           [vc_ref[pl.ds(h, tp, stride=HEADS), :].astype(BF16) for h in range(HEADS)],
           ctc_ref[...], None)

    @pl.when(j == pl.num_programs(1) - 1)
    def _():
        query = lax.broadcasted_iota(jnp.int32, (HEADS * tnew, tnew), 0) % tnew
        key = lax.broadcasted_iota(jnp.int32, (HEADS * tnew, tnew), 1)
        update([kn_ref[:, sl] for sl in heads], [vn_ref[:, sl] for sl in heads], ctn_ref[...], key <= query)
        out = acc_ref[...] / l_ref[...]
        for h in range(HEADS):
            o_ref[:, heads[h]] = out[h * tnew:(h + 1) * tnew, :].astype(BF16)


def _attn_sample(q, k_cache, v_cache, kb_new, vb_new, c_new, ct_cache, ct_new, *, tnew, tp=512):
    t, d = q.shape
    nb, past_rows, _ = k_cache.shape
    new_rows = pl.BlockSpec((tnew, d), lambda b, j: (b, 0))
    cache = pl.BlockSpec((None, tp * HEADS, HEAD_DIM), lambda b, j: (b, j, 0))
    return pl.pallas_call(
        functools.partial(_attn_sample_kernel, tp=tp),
        grid=(nb, past_rows // (tp * HEADS)),
        in_specs=[
            new_rows, cache, cache, new_rows, new_rows,
            pl.BlockSpec((tnew, LANES), lambda b, j: (b, 0)),
            pl.BlockSpec((None, HEADS, tp), lambda b, j: (b, 0, j)),
            pl.BlockSpec((None, HEADS, tnew), lambda b, j: (b, 0, 0)),
        ],
        out_specs=new_rows,
        out_shape=jax.ShapeDtypeStruct((t, d), BF16),
        scratch_shapes=[pltpu.VMEM((HEADS * tnew, LANES), F32), pltpu.VMEM((HEADS * tnew, LANES), F32),
                        pltpu.VMEM((HEADS * tnew, HEAD_DIM), F32), pltpu.VMEM((HEADS * tnew, LANES), F32)],
        compiler_params=_params(("parallel", "arbitrary")),
        name="attn_sample",
    )(q, k_cache, v_cache, kb_new, vb_new, c_new, ct_cache, ct_new)


def _oproj_kernel(x_ref, o_ref, wo_ref, g_ref, b_ref, out_ref):
    y = DN_ALPHA * x_ref[...] + jnp.dot(o_ref[...], wo_ref[...], preferred_element_type=F32)
    out_ref[...] = _layer_norm(y, g_ref[...], b_ref[...])


def _oproj_ln(x, o, wo, g, b, *, tm=512):
    t, d = x.shape
    row = pl.BlockSpec((tm, d), lambda i: (i, 0))
    vec = pl.BlockSpec((1, d), lambda i: (0, 0))
    return pl.pallas_call(
        _oproj_kernel,
        grid=(t // tm,),
        in_specs=[row, row, pl.BlockSpec((d, d), lambda i: (0, 0)), vec, vec],
        out_specs=row,
        out_shape=jax.ShapeDtypeStruct((t, d), F32),
        compiler_params=_params(("parallel",)),
        name="oproj_ln",
    )(x, o, wo, g, b)


def kernel(x_prompt, x_sample, state_pool, cache_fox_k, cache_fox_v, cache_fox_logf, ln_g, ln_b,
           ffn_w1, ffn_w3, ffn_w2, pool_w, pool_scale, fox_w_in, fox_b_f, fox_w_o):
    d = D_MODEL
    _, seq, _ = x_prompt.shape
    nb, tnew, _ = x_sample.shape
    past = cache_fox_k.shape[2]
    ns = nb * tnew

    def ffn(xp, xs, i, s, ln_idx):
        g, b = ln_g[i, ln_idx][None], ln_b[i, ln_idx][None]
        xs, w1b, w3b, w2b = _ffn_ln(xs, ffn_w1, ffn_w3, ffn_w2, g, b, tm=ns, tf=256, layer_half=(i, s))
        return _ffn_ln(xp, w1b, w3b, w2b, g, b, tm=512, tf=512), xs

    xp, xs = x_prompt.reshape(seq, d), x_sample.reshape(ns, d)

    xp, xs = ffn(xp, xs, 0, 0, 0)
    xp3, xs3 = xp.reshape(1, seq, d), xs.reshape(nb, tnew, d)
    pool_prompt = xp3[:, seq - POOL_STATE:][None]
    pool_sample = jnp.concatenate([state_pool[0], xs3], axis=1)[:, -POOL_STATE:][None]
    pw = pool_w[0].astype(BF16)
    ps, g1, b1 = pool_scale[0][None], ln_g[0, 1][None], ln_b[0, 1][None]
    tm = 512
    xp = _pool_ln(xp3, xp3, lambda bi, i: (bi, jnp.maximum(i * (tm // HALO) - 1, 0), 0), pw, ps, g1, b1,
                  bb=1, tm=tm, start_pos=0, zero_first=True).reshape(seq, d)
    prev = jnp.pad(state_pool[0], ((0, 0), (HALO - POOL_STATE, 0), (0, 0)))
    xs = _pool_ln(xs3, prev, lambda bi, i: (bi, 0, 0), pw, ps, g1, b1,
                  bb=nb, tm=tnew, start_pos=past, zero_first=False).reshape(ns, d)
    xp, xs = ffn(xp, xs, 0, 1, 2)

    xp, xs = ffn(xp, xs, 1, 0, 0)
    w_in = fox_w_in[0].astype(BF16)
    wf = jnp.pad(w_in[:, 3 * d:], ((0, 0), (0, LANES - HEADS)))
    bf = jnp.pad(fox_b_f[0], (0, LANES - HEADS))[None]
    q_p, k_p, v_p, kb_p, vb_p, lf_p, qmax, kmax = _fox_proj(xp, w_in, wf, bf)
    q_s, k_s, v_s, kb_s, vb_s, lf_s, _, _ = _fox_proj(xs, w_in, wf, bf)

    c_p = _cumsum_time(lf_p[None], tc=512)[0]
    o_p = _attn_prompt(q_p, kb_p, vb_p, c_p, qmax[:, 0, :HEADS], kmax[:, 0, :HEADS])

    lf_cache = jnp.pad(cache_fox_logf[0], ((0, 0), (0, 0), (0, LANES - HEADS)))
    lf_all = jnp.concatenate([lf_cache, lf_s.reshape(nb, tnew, LANES)], axis=1)
    c_s = _cumsum_time(lf_all, tc=(past + tnew) // 3)
    ct_s = c_s[:, :, :HEADS].transpose(0, 2, 1)
    o_s = _attn_sample(q_s, cache_fox_k[0].reshape(nb, past * HEADS, HEAD_DIM),
                       cache_fox_v[0].reshape(nb, past * HEADS, HEAD_DIM), kb_s, vb_s,
                       c_s[:, past:].reshape(ns, LANES), ct_s[:, :, :past], ct_s[:, :, past:], tnew=tnew)

    wo = fox_w_o[0].astype(BF16)
    g1, b1 = ln_g[1, 1][None], ln_b[1, 1][None]
    xp, xs = _oproj_ln(xp, o_p, wo, g1, b1), _oproj_ln(xs, o_s, wo, g1, b1)
    xp, xs = ffn(xp, xs, 1, 1, 2)

    shp = (HEADS, HEAD_DIM)
    return (xp.reshape(1, seq, d), xs.reshape(nb, tnew, d), pool_prompt, pool_sample,
            k_p.reshape(1, 1, seq, *shp), v_p.reshape(1, 1, seq, *shp), lf_p[:, :HEADS].reshape(1, 1, seq, HEADS),
            k_s.reshape(1, nb, tnew, *shp), v_s.reshape(1, nb, tnew, *shp),
            lf_s[:, :HEADS].reshape(1, nb, tnew, HEADS))
```

```python
import functools
import math

import jax
import jax.numpy as jnp
import numpy as np
from jax import lax
from jax.experimental import pallas as pl
from jax.experimental.pallas import tpu as pltpu

F32 = jnp.float32
BF16 = jnp.bfloat16

D_MODEL = 2048
DEPTH = 2
POOL_WINDOWS = (2, 4, 8, 16)
POOL_GROUP = D_MODEL // len(POOL_WINDOWS)
POOL_STATE = max(POOL_WINDOWS) - 1
HALO = POOL_STATE + 1
HEAD_DIM = 128
HEADS = D_MODEL // HEAD_DIM
LN_EPS = 1e-5
DN_ALPHA = (2 * DEPTH) ** 0.25
LOG2E = math.log2(math.e)
Q_SCALE = HEAD_DIM ** -0.5 * LOG2E
NEG_BIG = -1e30
SKIP_LOG2 = 150.0
LANES = 128
SUBLANES = 8
VMEM_LIMIT = 60 * 1024 * 1024


def _params(semantics):
    return pltpu.CompilerParams(dimension_semantics=semantics, vmem_limit_bytes=VMEM_LIMIT)


def _layer_norm(y, g, b):
    mu = jnp.mean(y, axis=-1, keepdims=True)
    yc = y - mu
    var = jnp.mean(yc * yc, axis=-1, keepdims=True)
    return yc * lax.rsqrt(var + LN_EPS) * g + b


def _ffn_kernel(x_ref, w1_ref, w3_ref, w2_ref, g_ref, b_ref, o_ref, *rest):
    *wb_refs, xb_ref = rest
    j = pl.program_id(1)

    @pl.when(j == 0)
    def _():
        xb_ref[...] = x_ref[...].astype(BF16)
        o_ref[...] = jnp.zeros_like(o_ref)

    w1, w3, w2 = w1_ref[...].astype(BF16), w3_ref[...].astype(BF16), w2_ref[...].astype(BF16)
    for wb_ref, w in zip(wb_refs, (w1, w3, w2)):
        wb_ref[...] = w
    xb = xb_ref[...]
    h1 = jnp.dot(xb, w1, preferred_element_type=F32)
    h3 = jnp.dot(xb, w3, preferred_element_type=F32)
    gate = (h1 * jax.nn.sigmoid(h1) * h3).astype(BF16)
    o_ref[...] += jnp.dot(gate, w2, preferred_element_type=F32)

    @pl.when(j == pl.num_programs(1) - 1)
    def _():
        y = DN_ALPHA * x_ref[...] + 0.5 * o_ref[...]
        o_ref[...] = _layer_norm(y, g_ref[...], b_ref[...])


def _ffn_ln(x, w1, w3, w2, g, b, *, tm, tf, layer_half=None):
    t, d = x.shape
    f = w1.shape[-1]
    emit = layer_half is not None
    assert not emit or t == tm

    def wspec(block, idx):
        if emit:
            return pl.BlockSpec((None, None) + block, lambda i, j: layer_half + idx(j))
        return pl.BlockSpec(block, lambda i, j: idx(j))

    up, down = wspec((d, tf), lambda j: (0, j)), wspec((tf, d), lambda j: (j, 0))
    out_specs = [pl.BlockSpec((tm, d), lambda i, j: (i, 0))]
    out_shape = [jax.ShapeDtypeStruct((t, d), F32)]
    if emit:
        out_specs += [pl.BlockSpec((d, tf), lambda i, j: (0, j)), pl.BlockSpec((d, tf), lambda i, j: (0, j)),
                      pl.BlockSpec((tf, d), lambda i, j: (j, 0))]
        out_shape += [jax.ShapeDtypeStruct((d, f), BF16), jax.ShapeDtypeStruct((d, f), BF16),
                      jax.ShapeDtypeStruct((f, d), BF16)]
    res = pl.pallas_call(
        _ffn_kernel,
        grid=(t // tm, f // tf),
        in_specs=[
            pl.BlockSpec((tm, d), lambda i, j: (i, 0)),
            up, up, down,
            pl.BlockSpec((1, d), lambda i, j: (0, 0)),
            pl.BlockSpec((1, d), lambda i, j: (0, 0)),
        ],
        out_specs=out_specs,
        out_shape=out_shape,
        scratch_shapes=[pltpu.VMEM((tm, d), BF16)],
        compiler_params=_params(("parallel", "arbitrary")),
        name="ffn_ln",
    )(x, w1, w3, w2, g, b)
    return res if emit else res[0]


def _pool_kernel(x_ref, halo_ref, pw_ref, ps_ref, g_ref, b_ref, o_ref, buf_ref, *, tm, start_pos, zero_first):
    i = pl.program_id(1)
    bb = x_ref.shape[0]
    x = x_ref[...]
    halo = halo_ref[...]
    if zero_first:
        halo = jnp.where(i == 0, 0.0, halo)
    pos = start_pos + i * tm + lax.broadcasted_iota(jnp.int32, (1, tm, 1), 1)
    ys = []
    for g, w in enumerate(POOL_WINDOWS):
        cols = slice(g * POOL_GROUP, (g + 1) * POOL_GROUP)
        xg = x[:, :, cols]
        buf_ref[:, 0:HALO, :] = halo[:, :, cols]
        buf_ref[:, HALO:, :] = xg
        win = xg
        for k in range(1, w):
            win = win + buf_ref[:, HALO - k:HALO - k + tm, :]
        cnt = jnp.minimum(pos + 1, w).astype(F32)
        diff = (win / cnt - xg).reshape(bb * tm, POOL_GROUP)
        ys.append(jnp.dot(diff.astype(BF16), pw_ref[g], preferred_element_type=F32))
    y = jnp.concatenate(ys, axis=-1) * ps_ref[...]
    out = _layer_norm(DN_ALPHA * x.reshape(bb * tm, D_MODEL) + y, g_ref[...], b_ref[...])
    o_ref[...] = out.reshape(bb, tm, D_MODEL)


def _pool_ln(x, halo_src, halo_map, pw, ps, g, b, *, bb, tm, start_pos, zero_first):
    nb, t, d = x.shape
    kern = functools.partial(_pool_kernel, tm=tm, start_pos=start_pos, zero_first=zero_first)
    return pl.pallas_call(
        kern,
        grid=(nb // bb, t // tm),
        in_specs=[
            pl.BlockSpec((bb, tm, d), lambda bi, i: (bi, i, 0)),
            pl.BlockSpec((bb, HALO, d), halo_map),
            pl.BlockSpec((len(POOL_WINDOWS), POOL_GROUP, POOL_GROUP), lambda bi, i: (0, 0, 0)),
            pl.BlockSpec((1, d), lambda bi, i: (0, 0)),
            pl.BlockSpec((1, d), lambda bi, i: (0, 0)),
            pl.BlockSpec((1, d), lambda bi, i: (0, 0)),
        ],
        out_specs=pl.BlockSpec((bb, tm, d), lambda bi, i: (bi, i, 0)),
        out_shape=jax.ShapeDtypeStruct((nb, t, d), F32),
        scratch_shapes=[pltpu.VMEM((bb, HALO + tm, POOL_GROUP), F32)],
        compiler_params=_params(("parallel", "arbitrary")),
        name="pool_ln",
    )(x, halo_src, pw, ps, g, b)


def _max_sq_norm_per_head(xb, first_head, acc):
    sq = xb.astype(F32)
    sq = sq * sq
    lane = lax.broadcasted_iota(jnp.int32, acc.shape, 1)
    for hh in range(xb.shape[1] // HEAD_DIM):
        row = jnp.sum(sq[:, hh * HEAD_DIM:(hh + 1) * HEAD_DIM], axis=-1, keepdims=True)
        acc = jnp.where(lane == first_head + hh, jnp.max(row, axis=0, keepdims=True), acc)
    return acc


def _proj_kernel(x_ref, wq_ref, wk_ref, wv_ref, wf_ref, bf_ref,
                 q_ref, k_ref, v_ref, kb_ref, vb_ref, lf_ref, qn_ref, kn_ref, xb_ref):
    j = pl.program_id(1)
    heads_per_step = q_ref.shape[1] // HEAD_DIM

    @pl.when(j == 0)
    def _():
        xb = x_ref[...].astype(BF16)
        xb_ref[...] = xb
        fl = jnp.dot(xb, wf_ref[...], preferred_element_type=F32) + bf_ref[...]
        lf_ref[...] = jnp.minimum(fl, 0.0) - jnp.log1p(jnp.exp(-jnp.abs(fl)))
        qn_ref[...] = jnp.zeros_like(qn_ref)
        kn_ref[...] = jnp.zeros_like(kn_ref)

    xb = xb_ref[...]
    qb = (jnp.dot(xb, wq_ref[...], preferred_element_type=F32) * Q_SCALE).astype(BF16)
    q_ref[...] = qb
    k = jnp.dot(xb, wk_ref[...], preferred_element_type=F32)
    kb = k.astype(BF16)
    kb_ref[...] = kb
    v = jnp.dot(xb, wv_ref[...], preferred_element_type=F32)
    vb_ref[...] = v.astype(BF16)
    tm = x_ref.shape[0]
    for hh in range(heads_per_step):
        rows = pl.ds(j * heads_per_step + hh, tm, stride=HEADS)
        k_ref[rows, :] = k[:, hh * HEAD_DIM:(hh + 1) * HEAD_DIM]
        v_ref[rows, :] = v[:, hh * HEAD_DIM:(hh + 1) * HEAD_DIM]
    qn_ref[...] = _max_sq_norm_per_head(qb, j * heads_per_step, qn_ref[...])
    kn_ref[...] = _max_sq_norm_per_head(kb, j * heads_per_step, kn_ref[...])


def _fox_proj(x, w_in, wf, bf, *, tm=512, tn=512):
    t, d = x.shape
    nd = d // tn
    row = pl.BlockSpec((tm, tn), lambda i, j: (i, j))

    def wcols(part):
        return pl.BlockSpec((d, tn), lambda i, j: (0, part * nd + j))

    wq, wk, wv = wcols(0), wcols(1), wcols(2)
    tile_stat = pl.BlockSpec((None, SUBLANES, LANES), lambda i, j: (i, 0, 0))
    by_head = pl.BlockSpec((tm * HEADS, HEAD_DIM), lambda i, j: (i, 0))
    return pl.pallas_call(
        _proj_kernel,
        grid=(t // tm, d // tn),
        in_specs=[
            pl.BlockSpec((tm, d), lambda i, j: (i, 0)),
            wq, wk, wv,
            pl.BlockSpec((d, LANES), lambda i, j: (0, 0)),
            pl.BlockSpec((1, LANES), lambda i, j: (0, 0)),
        ],
        out_specs=[row, by_head, by_head, row, row, pl.BlockSpec((tm, LANES), lambda i, j: (i, 0)), tile_stat, tile_stat],
        out_shape=[
            jax.ShapeDtypeStruct((t, d), BF16),
            jax.ShapeDtypeStruct((t * HEADS, HEAD_DIM), F32),
            jax.ShapeDtypeStruct((t * HEADS, HEAD_DIM), F32),
            jax.ShapeDtypeStruct((t, d), BF16),
            jax.ShapeDtypeStruct((t, d), BF16),
            jax.ShapeDtypeStruct((t, LANES), F32),
            jax.ShapeDtypeStruct((t // tm, SUBLANES, LANES), F32),
            jax.ShapeDtypeStruct((t // tm, SUBLANES, LANES), F32),
        ],
        scratch_shapes=[pltpu.VMEM((tm, d), BF16)],
        compiler_params=_params(("parallel", "arbitrary")),
        name="fox_proj",
    )(x, w_in, w_in, w_in, wf, bf)


def _split3(x):
    hi = x.astype(BF16)
    rem = x - hi.astype(F32)
    mid = rem.astype(BF16)
    lo = (rem - mid.astype(F32)).astype(BF16)
    return hi, mid, lo


def _cumsum_kernel(x_ref, o_ref, carry_ref, *, tc):
    @pl.when(pl.program_id(1) == 0)
    def _():
        carry_ref[...] = jnp.zeros_like(carry_ref)

    r = lax.broadcasted_iota(jnp.int32, (tc, tc), 0)
    c = lax.broadcasted_iota(jnp.int32, (tc, tc), 1)
    tri = (c <= r).astype(BF16)
    hi, mid, lo = _split3(x_ref[...])
    cs = (jnp.dot(tri, hi, preferred_element_type=F32)
          + jnp.dot(tri, mid, preferred_element_type=F32)
          + jnp.dot(tri, lo, preferred_element_type=F32)) + carry_ref[...]
    o_ref[...] = cs * LOG2E
    carry_ref[...] = cs[tc - 1:tc, :]


def _cumsum_time(x, *, tc):
    nb, t, w = x.shape
    return pl.pallas_call(
        functools.partial(_cumsum_kernel, tc=tc),
        grid=(nb, t // tc),
        in_specs=[pl.BlockSpec((None, tc, w), lambda b, j: (b, j, 0))],
        out_specs=pl.BlockSpec((None, tc, w), lambda b, j: (b, j, 0)),
        out_shape=jax.ShapeDtypeStruct((nb, t, w), F32),
        scratch_shapes=[pltpu.VMEM((1, w), F32)],
        compiler_params=_params(("parallel", "arbitrary")),
        name="cumsum_time",
    )(x)


N_SPLIT = 3
QK_AHEAD = 1


def _bias_selectors():
    sel_q = np.zeros((N_SPLIT * LANES, D_MODEL), np.float32)
    sel_k = np.zeros((N_SPLIT * LANES, D_MODEL), np.float32)
    one_q = np.zeros((1, D_MODEL), np.float32)
    one_k = np.zeros((1, D_MODEL), np.float32)
    for h in range(HEADS):
        for part in range(N_SPLIT):
            sel_q[part * LANES + h, h * HEAD_DIM + part] = 1.0
            sel_k[part * LANES + h, h * HEAD_DIM + N_SPLIT + part] = -1.0
            one_q[0, h * HEAD_DIM + N_SPLIT + part] = 1.0
            one_k[0, h * HEAD_DIM + part] = 1.0
    return jnp.asarray(sel_q, BF16), jnp.asarray(sel_k, BF16), jnp.asarray(one_q), jnp.asarray(one_k)


def _bias_operands_kernel(c_ref, sel_q_ref, sel_k_ref, one_q_ref, one_k_ref, a_ref, b_ref):
    parts = jnp.concatenate(_split3(c_ref[...]), axis=1)
    a_ref[...] = (jnp.dot(parts, sel_q_ref[...], preferred_element_type=F32) + one_q_ref[...]).astype(BF16)
    b_ref[...] = (jnp.dot(parts, sel_k_ref[...], preferred_element_type=F32) + one_k_ref[...]).astype(BF16)


def _bias_operands(c2, *, tr=1024):
    s, w = c2.shape
    rows = pl.BlockSpec((tr, D_MODEL), lambda i: (i, 0))
    sel = pl.BlockSpec((N_SPLIT * w, D_MODEL), lambda i: (0, 0))
    one = pl.BlockSpec((1, D_MODEL), lambda i: (0, 0))
    return pl.pallas_call(
        _bias_operands_kernel,
        grid=(s // tr,),
        in_specs=[pl.BlockSpec((tr, w), lambda i: (i, 0)), sel, sel, one, one],
        out_specs=[rows, rows],
        out_shape=[jax.ShapeDtypeStruct((s, D_MODEL), BF16), jax.ShapeDtypeStruct((s, D_MODEL), BF16)],
        compiler_params=_params(("parallel",)),
        name="bias_operands",
    )(c2, *_bias_selectors())


def _qk(q_h, k_h):
    return lax.dot_general(q_h, k_h, (((1,), (1,)), ((), ())), preferred_element_type=F32)


def _attn_schedule(qmax, kmax, c2, tq, nsteps):
    nq = qmax.shape[0]
    qn, kn = jnp.sqrt(qmax), jnp.sqrt(kmax)
    c_first, c_last = c2[0::tq, :HEADS], c2[tq - 1::tq, :HEADS]
    bound = qn[:, None, :] * (kn[None, :, :] + kn[:, None, :]) + c_first[:, None, :] - c_last[None, :, :]
    tile = jnp.arange(nq, dtype=jnp.int32)
    skip = jnp.all(bound < -SKIP_LOG2, axis=-1) & (tile[None, :] < tile[:, None])
    jstart = jnp.argmin(skip.astype(jnp.int32), axis=1).astype(jnp.int32)
    cnt = tile - jstart + 1
    ends = jnp.cumsum(cnt)
    starts = ends - cnt
    n = jnp.arange(nsteps, dtype=jnp.int32)
    valid = n < ends[-1]
    qi = jnp.minimum(jnp.sum((ends[None, :] <= n[:, None]).astype(jnp.int32), axis=1), nq - 1)
    kj = jstart[qi] + n - starts[qi]
    qi = jnp.where(valid, qi, nq - 1)
    kj = jnp.where(valid, kj, nq - 1)
    first = valid & (kj == jstart[qi])
    return qi, kj, valid.astype(jnp.int32) + 2 * first.astype(jnp.int32), ends[-1]


def _attn_prompt_kernel(qi_ref, kj_ref, fl_ref, q_ref, qc_ref, k_ref, kc_ref, v_ref, o_ref,
                        m_ref, l_ref, acc_ref, *, tq):
    n = pl.program_id(0)
    flags = fl_ref[n]
    nchunk = tq // LANES

    @pl.when(flags >= 2)
    def _():
        m_ref[...] = jnp.full_like(m_ref, NEG_BIG)
        l_ref[...] = jnp.zeros_like(l_ref)
        acc_ref[...] = jnp.zeros_like(acc_ref)

    def sweep(masked):
        if masked:
            row = lax.broadcasted_iota(jnp.int32, (tq, LANES), 0)
            lane = lax.broadcasted_iota(jnp.int32, (tq, LANES), 1)

        def logits(h):
            sl = slice(h * HEAD_DIM, (h + 1) * HEAD_DIM)
            return _qk(jnp.concatenate([q_ref[:, sl], qc_ref[:, sl]], axis=1),
                       jnp.concatenate([k_ref[:, sl], kc_ref[:, sl]], axis=1))

        pending = [logits(h) for h in range(QK_AHEAD)]
        for h in range(HEADS):
            sl = slice(h * HEAD_DIM, (h + 1) * HEAD_DIM)
            if h + QK_AHEAD < HEADS:
                pending.append(logits(h + QK_AHEAD))
            s = pending.pop(0)
            chunks = []
            for c in range(nchunk):
                sc = s[:, c * LANES:(c + 1) * LANES]
                if masked:
                    sc = jnp.where(lane + c * LANES <= row, sc, NEG_BIG)
                chunks.append(sc)
            mx = chunks[0]
            for sc in chunks[1:]:
                mx = jnp.maximum(mx, sc)
            m_prev = m_ref[h]
            m_new = jnp.maximum(m_prev, jnp.max(mx, axis=-1, keepdims=True))
            a = jnp.exp2(m_prev - m_new)
            ps = [jnp.exp2(sc - m_new) for sc in chunks]
            lsum = ps[0]
            for p in ps[1:]:
                lsum = lsum + p
            l_ref[h] = a * l_ref[h] + lsum
            p = jnp.concatenate(ps, axis=-1).astype(BF16)
            acc_ref[:, sl] = a * acc_ref[:, sl] + jnp.dot(p, v_ref[:, sl], preferred_element_type=F32)
            m_ref[h] = m_new

    diagonal = kj_ref[n] == qi_ref[n]

    @pl.when((flags >= 1) & jnp.logical_not(diagonal))
    def _():
        sweep(False)

    @pl.when((flags >= 1) & diagonal)
    def _():
        sweep(True)
        for h in range(HEADS):
            sl = slice(h * HEAD_DIM, (h + 1) * HEAD_DIM)
            l_tot = jnp.sum(l_ref[h], axis=-1, keepdims=True)
            o_ref[:, sl] = (acc_ref[:, sl] / l_tot).astype(BF16)


def _attn_prompt(q, kb, vb, c2, qmax, kmax, *, tq=512):
    s, d = q.shape
    nq = s // tq
    nsteps = nq * (nq + 1) // 2
    qi, kj, flags, nvalid = _attn_schedule(qmax, kmax, c2, tq, nsteps)
    qc, kc = _bias_operands(c2)
    qrow = pl.BlockSpec((tq, d), lambda n, qi, kj, fl: (qi[n], 0))
    krow = pl.BlockSpec((tq, d), lambda n, qi, kj, fl: (kj[n], 0))
    stat = pltpu.VMEM((HEADS, tq, LANES), F32)
    return pl.pallas_call(
        functools.partial(_attn_prompt_kernel, tq=tq),
        grid_spec=pltpu.PrefetchScalarGridSpec(
            num_scalar_prefetch=3,
            grid=(nvalid,),
            in_specs=[qrow, qrow, krow, krow, krow],
            out_specs=qrow,
            scratch_shapes=[stat, stat, pltpu.VMEM((tq, d), F32)],
        ),
        out_shape=jax.ShapeDtypeStruct((s, d), BF16),
        compiler_params=_params(("arbitrary",)),
        name="attn_prompt",
    )(qi, kj, flags, q, qc, kb, kc, vb)


def _attn_sample_kernel(q_ref, kc_ref, vc_ref, kn_ref, vn_ref, cq_ref, ctc_ref, ctn_ref, o_ref,
                        m_ref, l_ref, acc_ref, cqr_ref, *, tp):
    j = pl.program_id(1)
    tnew = q_ref.shape[0]
    heads = [slice(h * HEAD_DIM, (h + 1) * HEAD_DIM) for h in range(HEADS)]

    @pl.when(j == 0)
    def _():
        m_ref[...] = jnp.full_like(m_ref, NEG_BIG)
        l_ref[...] = jnp.zeros_like(l_ref)
        acc_ref[...] = jnp.zeros_like(acc_ref)
        for h in range(HEADS):
            cqr_ref[h * tnew:(h + 1) * tnew, :] = jnp.broadcast_to(cq_ref[:, h:h + 1], (tnew, LANES))

    def update(keys, values, key_c2, visible):
        n = key_c2.shape[1]
        s = jnp.concatenate([_qk(q_ref[:, heads[h]], keys[h]) for h in range(HEADS)], axis=0)
        c_keys = jnp.concatenate([jnp.broadcast_to(key_c2[h:h + 1, :], (tnew, n)) for h in range(HEADS)], axis=0)
        s = s + cqr_ref[:, :1] - c_keys
        if visible is not None:
            s = jnp.where(visible, s, NEG_BIG)
        m_prev = m_ref[...]
        m_new = jnp.maximum(m_prev, jnp.max(s, axis=-1, keepdims=True))
        a = jnp.exp2(m_prev - m_new)
        p = jnp.exp2(s - m_new[:, :1])
        l_ref[...] = a * l_ref[...] + jnp.sum(p, axis=-1, keepdims=True)
        pb = p.astype(BF16)
        pv = jnp.concatenate([jnp.dot(pb[h * tnew:(h + 1) * tnew], values[h], preferred_element_type=F32)
                              for h in range(HEADS)], axis=0)
        acc_ref[...] = a * acc_ref[...] + pv
        m_ref[...] = m_new

    update([kc_ref[pl.ds(h, tp, stride=HEADS), :].astype(BF16) for h in range(HEADS)],
           [vc_ref[pl.ds(h, tp, stride=HEADS), :].astype(BF16) for h in range(HEADS)],
           ctc_ref[...], None)

    @pl.when(j == pl.num_programs(1) - 1)
    def _():
        query = lax.broadcasted_iota(jnp.int32, (HEADS * tnew, tnew), 0) % tnew
        key = lax.broadcasted_iota(jnp.int32, (HEADS * tnew, tnew), 1)
        update([kn_ref[:, sl] for sl in heads], [vn_ref[:, sl] for sl in heads], ctn_ref[...], key <= query)
        out = acc_ref[...] / l_ref[...]
        for h in range(HEADS):
            o_ref[:, heads[h]] = out[h * tnew:(h + 1) * tnew, :].astype(BF16)


def _attn_sample(q, k_cache, v_cache, kb_new, vb_new, c_new, ct_cache, ct_new, *, tnew, tp=512):
    t, d = q.shape
    nb, past_rows, _ = k_cache.shape
    new_rows = pl.BlockSpec((tnew, d), lambda b, j: (b, 0))
    cache = pl.BlockSpec((None, tp * HEADS, HEAD_DIM), lambda b, j: (b, j, 0))
    return pl.pallas_call(
        functools.partial(_attn_sample_kernel, tp=tp),
        grid=(nb, past_rows // (tp * HEADS)),
        in_specs=[
            new_rows, cache, cache, new_rows, new_rows,
            pl.BlockSpec((tnew, LANES), lambda b, j: (b, 0)),
            pl.BlockSpec((None, HEADS, tp), lambda b, j: (b, 0, j)),
            pl.BlockSpec((None, HEADS, tnew), lambda b, j: (b, 0, 0)),
        ],
        out_specs=new_rows,
        out_shape=jax.ShapeDtypeStruct((t, d), BF16),
        scratch_shapes=[pltpu.VMEM((HEADS * tnew, LANES), F32), pltpu.VMEM((HEADS * tnew, LANES), F32),
                        pltpu.VMEM((HEADS * tnew, HEAD_DIM), F32), pltpu.VMEM((HEADS * tnew, LANES), F32)],
        compiler_params=_params(("parallel", "arbitrary")),
        name="attn_sample",
    )(q, k_cache, v_cache, kb_new, vb_new, c_new, ct_cache, ct_new)


def _oproj_kernel(x_ref, o_ref, wo_ref, g_ref, b_ref, out_ref):
    y = DN_ALPHA * x_ref[...] + jnp.dot(o_ref[...], wo_ref[...], preferred_element_type=F32)
    out_ref[...] = _layer_norm(y, g_ref[...], b_ref[...])


def _oproj_ln(x, o, wo, g, b, *, tm=512):
    t, d = x.shape
    row = pl.BlockSpec((tm, d), lambda i: (i, 0))
    vec = pl.BlockSpec((1, d), lambda i: (0, 0))
    return pl.pallas_call(
        _oproj_kernel,
        grid=(t // tm,),
        in_specs=[row, row, pl.BlockSpec((d, d), lambda i: (0, 0)), vec, vec],
        out_specs=row,
        out_shape=jax.ShapeDtypeStruct((t, d), F32),
        compiler_params=_params(("parallel",)),
        name="oproj_ln",
    )(x, o, wo, g, b)


def kernel(x_prompt, x_sample, state_pool, cache_fox_k, cache_fox_v, cache_fox_logf, ln_g, ln_b,
           ffn_w1, ffn_w3, ffn_w2, pool_w, pool_scale, fox_w_in, fox_b_f, fox_w_o):
    d = D_MODEL
    _, seq, _ = x_prompt.shape
    nb, tnew, _ = x_sample.shape
    past = cache_fox_k.shape[2]
    ns = nb * tnew

    def ffn(xp, xs, i, s, ln_idx):
        g, b = ln_g[i, ln_idx][None], ln_b[i, ln_idx][None]
        xs, w1b, w3b, w2b = _ffn_ln(xs, ffn_w1, ffn_w3, ffn_w2, g, b, tm=ns, tf=256, layer_half=(i, s))
        return _ffn_ln(xp, w1b, w3b, w2b, g, b, tm=512, tf=512), xs

    xp, xs = x_prompt.reshape(seq, d), x_sample.reshape(ns, d)

    xp, xs = ffn(xp, xs, 0, 0, 0)
    xp3, xs3 = xp.reshape(1, seq, d), xs.reshape(nb, tnew, d)
    pool_prompt = xp3[:, seq - POOL_STATE:][None]
    pool_sample = jnp.concatenate([state_pool[0], xs3], axis=1)[:, -POOL_STATE:][None]
    pw = pool_w[0].astype(BF16)
    ps, g1, b1 = pool_scale[0][None], ln_g[0, 1][None], ln_b[0, 1][None]
    tm = 512
    xp = _pool_ln(xp3, xp3, lambda bi, i: (bi, jnp.maximum(i * (tm // HALO) - 1, 0), 0), pw, ps, g1, b1,
                  bb=1, tm=tm, start_pos=0, zero_first=True).reshape(seq, d)
    prev = jnp.pad(state_pool[0], ((0, 0), (HALO - POOL_STATE, 0), (0, 0)))
    xs = _pool_ln(xs3, prev, lambda bi, i: (bi, 0, 0), pw, ps, g1, b1,
                  bb=nb, tm=tnew, start_pos=past, zero_first=False).reshape(ns, d)
    xp, xs = ffn(xp, xs, 0, 1, 2)

    xp, xs = ffn(xp, xs, 1, 0, 0)
    w_in = fox_w_in[0].astype(BF16)
    wf = jnp.pad(w_in[:, 3 * d:], ((0, 0), (0, LANES - HEADS)))
    bf = jnp.pad(fox_b_f[0], (0, LANES - HEADS))[None]
    q_p, k_p, v_p, kb_p, vb_p, lf_p, qmax, kmax = _fox_proj(xp, w_in, wf, bf)
    q_s, k_s, v_s, kb_s, vb_s, lf_s, _, _ = _fox_proj(xs, w_in, wf, bf)

    c_p = _cumsum_time(lf_p[None], tc=512)[0]
    o_p = _attn_prompt(q_p, kb_p, vb_p, c_p, qmax[:, 0, :HEADS], kmax[:, 0, :HEADS])

    total = past + tnew
    lf_all = jnp.concatenate([cache_fox_logf[0], lf_s[:, :HEADS].reshape(nb, tnew, HEADS)], axis=1)
    c_s = _cumsum_time(lf_all.transpose(1, 0, 2).reshape(1, total, nb * HEADS), tc=total // 3)
    c_s = c_s.reshape(total, nb, HEADS)
    ct_s = c_s.transpose(1, 2, 0)
    c_new = jnp.pad(c_s[past:].transpose(1, 0, 2).reshape(ns, HEADS), ((0, 0), (0, LANES - HEADS)))
    o_s = _attn_sample(q_s, cache_fox_k[0].reshape(nb, past * HEADS, HEAD_DIM),
                       cache_fox_v[0].reshape(nb, past * HEADS, HEAD_DIM), kb_s, vb_s,
                       c_new, ct_s[:, :, :past], ct_s[:, :, past:], tnew=tnew)

    wo = fox_w_o[0].astype(BF16)
    g1, b1 = ln_g[1, 1][None], ln_b[1, 1][None]
    xp, xs = _oproj_ln(xp, o_p, wo, g1, b1), _oproj_ln(xs, o_s, wo, g1, b1)
    xp, xs = ffn(xp, xs, 1, 1, 2)

    shp = (HEADS, HEAD_DIM)
    return (xp.reshape(1, seq, d), xs.reshape(nb, tnew, d), pool_prompt, pool_sample,
            k_p.reshape(1, 1, seq, *shp), v_p.reshape(1, 1, seq, *shp), lf_p[:, :HEADS].reshape(1, 1, seq, HEADS),
            k_s.reshape(1, nb, tnew, *shp), v_s.reshape(1, nb, tnew, *shp),
            lf_s[:, :HEADS].reshape(1, nb, tnew, HEADS))
```

```python
import functools
import math

import jax
import jax.numpy as jnp
import numpy as np
from jax import lax
from jax.experimental import pallas as pl
from jax.experimental.pallas import tpu as pltpu

F32 = jnp.float32
BF16 = jnp.bfloat16

D_MODEL = 2048
DEPTH = 2
POOL_WINDOWS = (2, 4, 8, 16)
POOL_GROUP = D_MODEL // len(POOL_WINDOWS)
POOL_STATE = max(POOL_WINDOWS) - 1
HALO = POOL_STATE + 1
HEAD_DIM = 128
HEADS = D_MODEL // HEAD_DIM
LN_EPS = 1e-5
DN_ALPHA = (2 * DEPTH) ** 0.25
LOG2E = math.log2(math.e)
Q_SCALE = HEAD_DIM ** -0.5 * LOG2E
NEG_BIG = -1e30
SKIP_LOG2 = 150.0
LANES = 128
SUBLANES = 8
VMEM_LIMIT = 60 * 1024 * 1024


def _params(semantics):
    return pltpu.CompilerParams(dimension_semantics=semantics, vmem_limit_bytes=VMEM_LIMIT)


def _layer_norm(y, g, b):
    mu = jnp.mean(y, axis=-1, keepdims=True)
    yc = y - mu
    var = jnp.mean(yc * yc, axis=-1, keepdims=True)
    return yc * lax.rsqrt(var + LN_EPS) * g + b


def _ffn_kernel(x_ref, w1_ref, w3_ref, w2_ref, g_ref, b_ref, o_ref, *rest):
    *wb_refs, xb_ref = rest
    j = pl.program_id(1)

    @pl.when(j == 0)
    def _():
        xb_ref[...] = x_ref[...].astype(BF16)
        o_ref[...] = jnp.zeros_like(o_ref)

    w1, w3, w2 = w1_ref[...].astype(BF16), w3_ref[...].astype(BF16), w2_ref[...].astype(BF16)
    for wb_ref, w in zip(wb_refs, (w1, w3, w2)):
        wb_ref[...] = w
    xb = xb_ref[...]
    h1 = jnp.dot(xb, w1, preferred_element_type=F32)
    h3 = jnp.dot(xb, w3, preferred_element_type=F32)
    gate = (h1 * jax.nn.sigmoid(h1) * h3).astype(BF16)
    o_ref[...] += jnp.dot(gate, w2, preferred_element_type=F32)

    @pl.when(j == pl.num_programs(1) - 1)
    def _():
        y = DN_ALPHA * x_ref[...] + 0.5 * o_ref[...]
        o_ref[...] = _layer_norm(y, g_ref[...], b_ref[...])


def _ffn_ln(x, w1, w3, w2, g, b, *, tm, tf, layer_half=None):
    t, d = x.shape
    f = w1.shape[-1]
    emit = layer_half is not None
    assert not emit or t == tm

    def wspec(block, idx):
        if emit:
            return pl.BlockSpec((None, None) + block, lambda i, j: layer_half + idx(j))
        return pl.BlockSpec(block, lambda i, j: idx(j))

    up, down = wspec((d, tf), lambda j: (0, j)), wspec((tf, d), lambda j: (j, 0))
    out_specs = [pl.BlockSpec((tm, d), lambda i, j: (i, 0))]
    out_shape = [jax.ShapeDtypeStruct((t, d), F32)]
    if emit:
        out_specs += [pl.BlockSpec((d, tf), lambda i, j: (0, j)), pl.BlockSpec((d, tf), lambda i, j: (0, j)),
                      pl.BlockSpec((tf, d), lambda i, j: (j, 0))]
        out_shape += [jax.ShapeDtypeStruct((d, f), BF16), jax.ShapeDtypeStruct((d, f), BF16),
                      jax.ShapeDtypeStruct((f, d), BF16)]
    res = pl.pallas_call(
        _ffn_kernel,
        grid=(t // tm, f // tf),
        in_specs=[
            pl.BlockSpec((tm, d), lambda i, j: (i, 0)),
            up, up, down,
            pl.BlockSpec((1, d), lambda i, j: (0, 0)),
            pl.BlockSpec((1, d), lambda i, j: (0, 0)),
        ],
        out_specs=out_specs,
        out_shape=out_shape,
        scratch_shapes=[pltpu.VMEM((tm, d), BF16)],
        compiler_params=_params(("parallel", "arbitrary")),
        name="ffn_ln",
    )(x, w1, w3, w2, g, b)
    return res if emit else res[0]


def _pool_kernel(x_ref, halo_ref, pw_ref, ps_ref, g_ref, b_ref, o_ref, buf_ref, *, tm, start_pos, zero_first):
    i = pl.program_id(1)
    bb = x_ref.shape[0]
    x = x_ref[...]
    halo = halo_ref[...]
    if zero_first:
        halo = jnp.where(i == 0, 0.0, halo)
    pos = start_pos + i * tm + lax.broadcasted_iota(jnp.int32, (1, tm, 1), 1)
    ys = []
    for g, w in enumerate(POOL_WINDOWS):
        cols = slice(g * POOL_GROUP, (g + 1) * POOL_GROUP)
        xg = x[:, :, cols]
        buf_ref[:, 0:HALO, :] = halo[:, :, cols]
        buf_ref[:, HALO:, :] = xg
        win = xg
        for k in range(1, w):
            win = win + buf_ref[:, HALO - k:HALO - k + tm, :]
        cnt = jnp.minimum(pos + 1, w).astype(F32)
        diff = (win / cnt - xg).reshape(bb * tm, POOL_GROUP)
        ys.append(jnp.dot(diff.astype(BF16), pw_ref[g], preferred_element_type=F32))
    y = jnp.concatenate(ys, axis=-1) * ps_ref[...]
    out = _layer_norm(DN_ALPHA * x.reshape(bb * tm, D_MODEL) + y, g_ref[...], b_ref[...])
    o_ref[...] = out.reshape(bb, tm, D_MODEL)


def _pool_ln(x, halo_src, halo_map, pw, ps, g, b, *, bb, tm, start_pos, zero_first):
    nb, t, d = x.shape
    kern = functools.partial(_pool_kernel, tm=tm, start_pos=start_pos, zero_first=zero_first)
    return pl.pallas_call(
        kern,
        grid=(nb // bb, t // tm),
        in_specs=[
            pl.BlockSpec((bb, tm, d), lambda bi, i: (bi, i, 0)),
            pl.BlockSpec((bb, HALO, d), halo_map),
            pl.BlockSpec((len(POOL_WINDOWS), POOL_GROUP, POOL_GROUP), lambda bi, i: (0, 0, 0)),
            pl.BlockSpec((1, d), lambda bi, i: (0, 0)),
            pl.BlockSpec((1, d), lambda bi, i: (0, 0)),
            pl.BlockSpec((1, d), lambda bi, i: (0, 0)),
        ],
        out_specs=pl.BlockSpec((bb, tm, d), lambda bi, i: (bi, i, 0)),
        out_shape=jax.ShapeDtypeStruct((nb, t, d), F32),
        scratch_shapes=[pltpu.VMEM((bb, HALO + tm, POOL_GROUP), F32)],
        compiler_params=_params(("parallel", "arbitrary")),
        name="pool_ln",
    )(x, halo_src, pw, ps, g, b)


def _max_sq_norm_per_head(xb, first_head, acc):
    sq = xb.astype(F32)
    sq = sq * sq
    lane = lax.broadcasted_iota(jnp.int32, acc.shape, 1)
    for hh in range(xb.shape[1] // HEAD_DIM):
        row = jnp.sum(sq[:, hh * HEAD_DIM:(hh + 1) * HEAD_DIM], axis=-1, keepdims=True)
        acc = jnp.where(lane == first_head + hh, jnp.max(row, axis=0, keepdims=True), acc)
    return acc


def _proj_kernel(x_ref, wq_ref, wk_ref, wv_ref, wf_ref, bf_ref,
                 q_ref, k_ref, v_ref, kb_ref, vb_ref, lf_ref, qn_ref, kn_ref, xb_ref):
    j = pl.program_id(1)
    heads_per_step = q_ref.shape[1] // HEAD_DIM

    @pl.when(j == 0)
    def _():
        xb = x_ref[...].astype(BF16)
        xb_ref[...] = xb
        fl = jnp.dot(xb, wf_ref[...], preferred_element_type=F32) + bf_ref[...]
        lf_ref[...] = jnp.minimum(fl, 0.0) - jnp.log1p(jnp.exp(-jnp.abs(fl)))
        qn_ref[...] = jnp.zeros_like(qn_ref)
        kn_ref[...] = jnp.zeros_like(kn_ref)

    xb = xb_ref[...]
    qb = (jnp.dot(xb, wq_ref[...], preferred_element_type=F32) * Q_SCALE).astype(BF16)
    q_ref[...] = qb
    k = jnp.dot(xb, wk_ref[...], preferred_element_type=F32)
    kb = k.astype(BF16)
    kb_ref[...] = kb
    v = jnp.dot(xb, wv_ref[...], preferred_element_type=F32)
    vb_ref[...] = v.astype(BF16)
    tm = x_ref.shape[0]
    for hh in range(heads_per_step):
        rows = pl.ds(j * heads_per_step + hh, tm, stride=HEADS)
        k_ref[rows, :] = k[:, hh * HEAD_DIM:(hh + 1) * HEAD_DIM]
        v_ref[rows, :] = v[:, hh * HEAD_DIM:(hh + 1) * HEAD_DIM]
    qn_ref[...] = _max_sq_norm_per_head(qb, j * heads_per_step, qn_ref[...])
    kn_ref[...] = _max_sq_norm_per_head(kb, j * heads_per_step, kn_ref[...])


def _fox_proj(x, w_in, wf, bf, *, tm=512, tn=512):
    t, d = x.shape
    nd = d // tn
    row = pl.BlockSpec((tm, tn), lambda i, j: (i, j))

    def wcols(part):
        return pl.BlockSpec((d, tn), lambda i, j: (0, part * nd + j))

    wq, wk, wv = wcols(0), wcols(1), wcols(2)
    tile_stat = pl.BlockSpec((None, SUBLANES, LANES), lambda i, j: (i, 0, 0))
    by_head = pl.BlockSpec((tm * HEADS, HEAD_DIM), lambda i, j: (i, 0))
    return pl.pallas_call(
        _proj_kernel,
        grid=(t // tm, d // tn),
        in_specs=[
            pl.BlockSpec((tm, d), lambda i, j: (i, 0)),
            wq, wk, wv,
            pl.BlockSpec((d, LANES), lambda i, j: (0, 0)),
            pl.BlockSpec((1, LANES), lambda i, j: (0, 0)),
        ],
        out_specs=[row, by_head, by_head, row, row, pl.BlockSpec((tm, LANES), lambda i, j: (i, 0)), tile_stat, tile_stat],
        out_shape=[
            jax.ShapeDtypeStruct((t, d), BF16),
            jax.ShapeDtypeStruct((t * HEADS, HEAD_DIM), F32),
            jax.ShapeDtypeStruct((t * HEADS, HEAD_DIM), F32),
            jax.ShapeDtypeStruct((t, d), BF16),
            jax.ShapeDtypeStruct((t, d), BF16),
            jax.ShapeDtypeStruct((t, LANES), F32),
            jax.ShapeDtypeStruct((t // tm, SUBLANES, LANES), F32),
            jax.ShapeDtypeStruct((t // tm, SUBLANES, LANES), F32),
        ],
        scratch_shapes=[pltpu.VMEM((tm, d), BF16)],
        compiler_params=_params(("parallel", "arbitrary")),
        name="fox_proj",
    )(x, w_in, w_in, w_in, wf, bf)


def _split3(x):
    hi = x.astype(BF16)
    rem = x - hi.astype(F32)
    mid = rem.astype(BF16)
    lo = (rem - mid.astype(F32)).astype(BF16)
    return hi, mid, lo


def _cumsum_kernel(x_ref, o_ref, carry_ref, *, tc):
    @pl.when(pl.program_id(1) == 0)
    def _():
        carry_ref[...] = jnp.zeros_like(carry_ref)

    r = lax.broadcasted_iota(jnp.int32, (tc, tc), 0)
    c = lax.broadcasted_iota(jnp.int32, (tc, tc), 1)
    tri = (c <= r).astype(BF16)
    hi, mid, lo = _split3(x_ref[...])
    cs = (jnp.dot(tri, hi, preferred_element_type=F32)
          + jnp.dot(tri, mid, preferred_element_type=F32)
          + jnp.dot(tri, lo, preferred_element_type=F32)) + carry_ref[...]
    o_ref[...] = cs * LOG2E
    carry_ref[...] = cs[tc - 1:tc, :]


def _cumsum_time(x, *, tc):
    nb, t, w = x.shape
    return pl.pallas_call(
        functools.partial(_cumsum_kernel, tc=tc),
        grid=(nb, t // tc),
        in_specs=[pl.BlockSpec((None, tc, w), lambda b, j: (b, j, 0))],
        out_specs=pl.BlockSpec((None, tc, w), lambda b, j: (b, j, 0)),
        out_shape=jax.ShapeDtypeStruct((nb, t, w), F32),
        scratch_shapes=[pltpu.VMEM((1, w), F32)],
        compiler_params=_params(("parallel", "arbitrary")),
        name="cumsum_time",
    )(x)


N_SPLIT = 3
QK_AHEAD = 1


def _bias_selectors():
    sel_q = np.zeros((N_SPLIT * LANES, D_MODEL), np.float32)
    sel_k = np.zeros((N_SPLIT * LANES, D_MODEL), np.float32)
    one_q = np.zeros((1, D_MODEL), np.float32)
    one_k = np.zeros((1, D_MODEL), np.float32)
    for h in range(HEADS):
        for part in range(N_SPLIT):
            sel_q[part * LANES + h, h * HEAD_DIM + part] = 1.0
            sel_k[part * LANES + h, h * HEAD_DIM + N_SPLIT + part] = -1.0
            one_q[0, h * HEAD_DIM + N_SPLIT + part] = 1.0
            one_k[0, h * HEAD_DIM + part] = 1.0
    return jnp.asarray(sel_q, BF16), jnp.asarray(sel_k, BF16), jnp.asarray(one_q), jnp.asarray(one_k)


def _bias_operands_kernel(c_ref, sel_q_ref, sel_k_ref, one_q_ref, one_k_ref, a_ref, b_ref):
    parts = jnp.concatenate(_split3(c_ref[...]), axis=1)
    a_ref[...] = (jnp.dot(parts, sel_q_ref[...], preferred_element_type=F32) + one_q_ref[...]).astype(BF16)
    b_ref[...] = (jnp.dot(parts, sel_k_ref[...], preferred_element_type=F32) + one_k_ref[...]).astype(BF16)


def _bias_operands(c2, *, tr=1024):
    s, w = c2.shape
    rows = pl.BlockSpec((tr, D_MODEL), lambda i: (i, 0))
    sel = pl.BlockSpec((N_SPLIT * w, D_MODEL), lambda i: (0, 0))
    one = pl.BlockSpec((1, D_MODEL), lambda i: (0, 0))
    return pl.pallas_call(
        _bias_operands_kernel,
        grid=(s // tr,),
        in_specs=[pl.BlockSpec((tr, w), lambda i: (i, 0)), sel, sel, one, one],
        out_specs=[rows, rows],
        out_shape=[jax.ShapeDtypeStruct((s, D_MODEL), BF16), jax.ShapeDtypeStruct((s, D_MODEL), BF16)],
        compiler_params=_params(("parallel",)),
        name="bias_operands",
    )(c2, *_bias_selectors())


def _qk(q_h, k_h):
    return lax.dot_general(q_h, k_h, (((1,), (1,)), ((), ())), preferred_element_type=F32)


def _attn_schedule(qmax, kmax, c2, tq, nsteps):
    nq = qmax.shape[0]
    qn, kn = jnp.sqrt(qmax), jnp.sqrt(kmax)
    c_first, c_last = c2[0::tq, :HEADS], c2[tq - 1::tq, :HEADS]
    bound = qn[:, None, :] * (kn[None, :, :] + kn[:, None, :]) + c_first[:, None, :] - c_last[None, :, :]
    tile = jnp.arange(nq, dtype=jnp.int32)
    skip = jnp.all(bound < -SKIP_LOG2, axis=-1) & (tile[None, :] < tile[:, None])
    jstart = jnp.argmin(skip.astype(jnp.int32), axis=1).astype(jnp.int32)
    cnt = tile - jstart + 1
    ends = jnp.cumsum(cnt)
    starts = ends - cnt
    n = jnp.arange(nsteps, dtype=jnp.int32)
    valid = n < ends[-1]
    qi = jnp.minimum(jnp.sum((ends[None, :] <= n[:, None]).astype(jnp.int32), axis=1), nq - 1)
    kj = jstart[qi] + n - starts[qi]
    qi = jnp.where(valid, qi, nq - 1)
    kj = jnp.where(valid, kj, nq - 1)
    first = valid & (kj == jstart[qi])
    return qi, kj, valid.astype(jnp.int32) + 2 * first.astype(jnp.int32)


def _attn_prompt_kernel(qi_ref, kj_ref, fl_ref, q_ref, qc_ref, k_ref, kc_ref, v_ref, o_ref,
                        m_ref, l_ref, acc_ref, *, tq):
    n = pl.program_id(0)
    flags = fl_ref[n]
    nchunk = tq // LANES

    @pl.when(flags >= 2)
    def _():
        m_ref[...] = jnp.full_like(m_ref, NEG_BIG)
        l_ref[...] = jnp.zeros_like(l_ref)
        acc_ref[...] = jnp.zeros_like(acc_ref)

    def sweep(masked):
        if masked:
            row = lax.broadcasted_iota(jnp.int32, (tq, LANES), 0)
            lane = lax.broadcasted_iota(jnp.int32, (tq, LANES), 1)

        def logits(h):
            sl = slice(h * HEAD_DIM, (h + 1) * HEAD_DIM)
            return _qk(jnp.concatenate([q_ref[:, sl], qc_ref[:, sl]], axis=1),
                       jnp.concatenate([k_ref[:, sl], kc_ref[:, sl]], axis=1))

        pending = [logits(h) for h in range(QK_AHEAD)]
        for h in range(HEADS):
            sl = slice(h * HEAD_DIM, (h + 1) * HEAD_DIM)
            if h + QK_AHEAD < HEADS:
                pending.append(logits(h + QK_AHEAD))
            s = pending.pop(0)
            chunks = []
            for c in range(nchunk):
                sc = s[:, c * LANES:(c + 1) * LANES]
                if masked:
                    sc = jnp.where(lane + c * LANES <= row, sc, NEG_BIG)
                chunks.append(sc)
            mx = chunks[0]
            for sc in chunks[1:]:
                mx = jnp.maximum(mx, sc)
            m_prev = m_ref[h]
            m_new = jnp.maximum(m_prev, jnp.max(mx, axis=-1, keepdims=True))
            a = jnp.exp2(m_prev - m_new)
            ps = [jnp.exp2(sc - m_new) for sc in chunks]
            lsum = ps[0]
            for p in ps[1:]:
                lsum = lsum + p
            l_ref[h] = a * l_ref[h] + lsum
            p = jnp.concatenate(ps, axis=-1).astype(BF16)
            acc_ref[:, sl] = a * acc_ref[:, sl] + jnp.dot(p, v_ref[:, sl], preferred_element_type=F32)
            m_ref[h] = m_new

    diagonal = kj_ref[n] == qi_ref[n]

    @pl.when((flags >= 1) & jnp.logical_not(diagonal))
    def _():
        sweep(False)

    @pl.when((flags >= 1) & diagonal)
    def _():
        sweep(True)
        for h in range(HEADS):
            sl = slice(h * HEAD_DIM, (h + 1) * HEAD_DIM)
            l_tot = jnp.sum(l_ref[h], axis=-1, keepdims=True)
            o_ref[:, sl] = (acc_ref[:, sl] / l_tot).astype(BF16)


def _attn_prompt(q, kb, vb, c2, qmax, kmax, *, tq=512):
    s, d = q.shape
    nq = s // tq
    nsteps = nq * (nq + 1) // 2
    qi, kj, flags = _attn_schedule(qmax, kmax, c2, tq, nsteps)
    qc, kc = _bias_operands(c2)
    qrow = pl.BlockSpec((tq, d), lambda n, qi, kj, fl: (qi[n], 0))
    krow = pl.BlockSpec((tq, d), lambda n, qi, kj, fl: (kj[n], 0))
    stat = pltpu.VMEM((HEADS, tq, LANES), F32)
    return pl.pallas_call(
        functools.partial(_attn_prompt_kernel, tq=tq),
        grid_spec=pltpu.PrefetchScalarGridSpec(
            num_scalar_prefetch=3,
            grid=(nsteps,),
            in_specs=[qrow, qrow, krow, krow, krow],
            out_specs=qrow,
            scratch_shapes=[stat, stat, pltpu.VMEM((tq, d), F32)],
        ),
        out_shape=jax.ShapeDtypeStruct((s, d), BF16),
        compiler_params=_params(("arbitrary",)),
        name="attn_prompt",
    )(qi, kj, flags, q, qc, kb, kc, vb)


def _attn_sample_kernel(q_ref, kc_ref, vc_ref, kn_ref, vn_ref, cq_ref, ctc_ref, ctn_ref, o_ref,
                        m_ref, l_ref, acc_ref, cqr_ref, *, tp):
    j = pl.program_id(1)
    tnew = q_ref.shape[0]
    heads = [slice(h * HEAD_DIM, (h + 1) * HEAD_DIM) for h in range(HEADS)]

    @pl.when(j == 0)
    def _():
        m_ref[...] = jnp.full_like(m_ref, NEG_BIG)
        l_ref[...] = jnp.zeros_like(l_ref)
        acc_ref[...] = jnp.zeros_like(acc_ref)
        for h in range(HEADS):
            cqr_ref[h * tnew:(h + 1) * tnew, :] = jnp.broadcast_to(cq_ref[:, h:h + 1], (tnew, LANES))

    def update(keys, values, key_c2, visible):
        n = key_c2.shape[1]
        s = jnp.concatenate([_qk(q_ref[:, heads[h]], keys[h]) for h in range(HEADS)], axis=0)
        c_keys = jnp.concatenate([jnp.broadcast_to(key_c2[h:h + 1, :], (tnew, n)) for h in range(HEADS)], axis=0)
        s = s + cqr_ref[:, :1] - c_keys
        if visible is not None:
            s = jnp.where(visible, s, NEG_BIG)
        m_prev = m_ref[...]
        m_new = jnp.maximum(m_prev, jnp.max(s, axis=-1, keepdims=True))
        a = jnp.exp2(m_prev - m_new)
        p = jnp.exp2(s - m_new[:, :1])
        l_ref[...] = a * l_ref[...] + jnp.sum(p, axis=-1, keepdims=True)
        pb = p.astype(BF16)
        pv = jnp.concatenate([jnp.dot(pb[h * tnew:(h + 1) * tnew], values[h], preferred_element_type=F32)
                              for h in range(HEADS)], axis=0)
        acc_ref[...] = a * acc_ref[...] + pv
        m_ref[...] = m_new

    update([kc_ref[pl.ds(h, tp, stride=HEADS), :].astype(BF16) for h in range(HEADS)],
           [vc_ref[pl.ds(h, tp, stride=HEADS), :].astype(BF16) for h in range(HEADS)],
           ctc_ref[...], None)

    @pl.when(j == pl.num_programs(1) - 1)
    def _():
        query = lax.broadcasted_iota(jnp.int32, (HEADS * tnew, tnew), 0) % tnew
        key = lax.broadcasted_iota(jnp.int32, (HEADS * tnew, tnew), 1)
        update([kn_ref[:, sl] for sl in heads], [vn_ref[:, sl] for sl in heads], ctn_ref[...], key <= query)
        out = acc_ref[...] / l_ref[...]
        for h in range(HEADS):
            o_ref[:, heads[h]] = out[h * tnew:(h + 1) * tnew, :].astype(BF16)


def _attn_sample(q, k_cache, v_cache, kb_new, vb_new, c_new, ct_cache, ct_new, *, tnew, tp=512):
    t, d = q.shape
    nb, past_rows, _ = k_cache.shape
    new_rows = pl.BlockSpec((tnew, d), lambda b, j: (b, 0))
    cache = pl.BlockSpec((None, tp * HEADS, HEAD_DIM), lambda b, j: (b, j, 0))
    return pl.pallas_call(
        functools.partial(_attn_sample_kernel, tp=tp),
        grid=(nb, past_rows // (tp * HEADS)),
        in_specs=[
            new_rows, cache, cache, new_rows, new_rows,
            pl.BlockSpec((tnew, LANES), lambda b, j: (b, 0)),
            pl.BlockSpec((None, HEADS, tp), lambda b, j: (b, 0, j)),
            pl.BlockSpec((None, HEADS, tnew), lambda b, j: (b, 0, 0)),
        ],
        out_specs=new_rows,
        out_shape=jax.ShapeDtypeStruct((t, d), BF16),
        scratch_shapes=[pltpu.VMEM((HEADS * tnew, LANES), F32), pltpu.VMEM((HEADS * tnew, LANES), F32),
                        pltpu.VMEM((HEADS * tnew, HEAD_DIM), F32), pltpu.VMEM((HEADS * tnew, LANES), F32)],
        compiler_params=_params(("parallel", "arbitrary")),
        name="attn_sample",
    )(q, k_cache, v_cache, kb_new, vb_new, c_new, ct_cache, ct_new)


def _oproj_kernel(x_ref, o_ref, wo_ref, g_ref, b_ref, out_ref):
    y = DN_ALPHA * x_ref[...] + jnp.dot(o_ref[...], wo_ref[...], preferred_element_type=F32)
    out_ref[...] = _layer_norm(y, g_ref[...], b_ref[...])


def _oproj_ln(x, o, wo, g, b, *, tm=512):
    t, d = x.shape
    row = pl.BlockSpec((tm, d), lambda i: (i, 0))
    vec = pl.BlockSpec((1, d), lambda i: (0, 0))
    return pl.pallas_call(
        _oproj_kernel,
        grid=(t // tm,),
        in_specs=[row, row, pl.BlockSpec((d, d), lambda i: (0, 0)), vec, vec],
        out_specs=row,
        out_shape=jax.ShapeDtypeStruct((t, d), F32),
        compiler_params=_params(("parallel",)),
        name="oproj_ln",
    )(x, o, wo, g, b)


def kernel(x_prompt, x_sample, state_pool, cache_fox_k, cache_fox_v, cache_fox_logf, ln_g, ln_b,
           ffn_w1, ffn_w3, ffn_w2, pool_w, pool_scale, fox_w_in, fox_b_f, fox_w_o):
    d = D_MODEL
    _, seq, _ = x_prompt.shape
    nb, tnew, _ = x_sample.shape
    past = cache_fox_k.shape[2]
    ns = nb * tnew

    def ffn(xp, xs, i, s, ln_idx):
        g, b = ln_g[i, ln_idx][None], ln_b[i, ln_idx][None]
        xs, w1b, w3b, w2b = _ffn_ln(xs, ffn_w1, ffn_w3, ffn_w2, g, b, tm=ns, tf=256, layer_half=(i, s))
        return _ffn_ln(xp, w1b, w3b, w2b, g, b, tm=512, tf=1024), xs

    xp, xs = x_prompt.reshape(seq, d), x_sample.reshape(ns, d)

    xp, xs = ffn(xp, xs, 0, 0, 0)
    xp3, xs3 = xp.reshape(1, seq, d), xs.reshape(nb, tnew, d)
    pool_prompt = xp3[:, seq - POOL_STATE:][None]
    pool_sample = jnp.concatenate([state_pool[0], xs3], axis=1)[:, -POOL_STATE:][None]
    pw = pool_w[0].astype(BF16)
    ps, g1, b1 = pool_scale[0][None], ln_g[0, 1][None], ln_b[0, 1][None]
    tm = 512
    xp = _pool_ln(xp3, xp3, lambda bi, i: (bi, jnp.maximum(i * (tm // HALO) - 1, 0), 0), pw, ps, g1, b1,
                  bb=1, tm=tm, start_pos=0, zero_first=True).reshape(seq, d)
    prev = jnp.pad(state_pool[0], ((0, 0), (HALO - POOL_STATE, 0), (0, 0)))
    xs = _pool_ln(xs3, prev, lambda bi, i: (bi, 0, 0), pw, ps, g1, b1,
                  bb=nb, tm=tnew, start_pos=past, zero_first=False).reshape(ns, d)
    xp, xs = ffn(xp, xs, 0, 1, 2)

    xp, xs = ffn(xp, xs, 1, 0, 0)
    w_in = fox_w_in[0].astype(BF16)
    wf = jnp.pad(w_in[:, 3 * d:], ((0, 0), (0, LANES - HEADS)))
    bf = jnp.pad(fox_b_f[0], (0, LANES - HEADS))[None]
    q_p, k_p, v_p, kb_p, vb_p, lf_p, qmax, kmax = _fox_proj(xp, w_in, wf, bf)
    q_s, k_s, v_s, kb_s, vb_s, lf_s, _, _ = _fox_proj(xs, w_in, wf, bf)

    c_p = _cumsum_time(lf_p[None], tc=512)[0]
    o_p = _attn_prompt(q_p, kb_p, vb_p, c_p, qmax[:, 0, :HEADS], kmax[:, 0, :HEADS])

    total = past + tnew
    lf_all = jnp.concatenate([cache_fox_logf[0], lf_s[:, :HEADS].reshape(nb, tnew, HEADS)], axis=1)
    c_s = _cumsum_time(lf_all.transpose(1, 0, 2).reshape(1, total, nb * HEADS), tc=total // 3)
    c_s = c_s.reshape(total, nb, HEADS)
    ct_s = c_s.transpose(1, 2, 0)
    c_new = jnp.pad(c_s[past:].transpose(1, 0, 2).reshape(ns, HEADS), ((0, 0), (0, LANES - HEADS)))
    o_s = _attn_sample(q_s, cache_fox_k[0].reshape(nb, past * HEADS, HEAD_DIM),
                       cache_fox_v[0].reshape(nb, past * HEADS, HEAD_DIM), kb_s, vb_s,
                       c_new, ct_s[:, :, :past], ct_s[:, :, past:], tnew=tnew)

    wo = fox_w_o[0].astype(BF16)
    g1, b1 = ln_g[1, 1][None], ln_b[1, 1][None]
    xp, xs = _oproj_ln(xp, o_p, wo, g1, b1), _oproj_ln(xs, o_s, wo, g1, b1)
    xp, xs = ffn(xp, xs, 1, 1, 2)

    shp = (HEADS, HEAD_DIM)
    return (xp.reshape(1, seq, d), xs.reshape(nb, tnew, d), pool_prompt, pool_sample,
            k_p.reshape(1, 1, seq, *shp), v_p.reshape(1, 1, seq, *shp), lf_p[:, :HEADS].reshape(1, 1, seq, HEADS),
            k_s.reshape(1, nb, tnew, *shp), v_s.reshape(1, nb, tnew, *shp),
            lf_s[:, :HEADS].reshape(1, nb, tnew, HEADS))
```

```python
import functools
import math

import jax
import jax.numpy as jnp
import numpy as np
from jax import lax
from jax.experimental import pallas as pl
from jax.experimental.pallas import tpu as pltpu

F32 = jnp.float32
BF16 = jnp.bfloat16

D_MODEL = 2048
DEPTH = 2
POOL_WINDOWS = (2, 4, 8, 16)
POOL_GROUP = D_MODEL // len(POOL_WINDOWS)
POOL_STATE = max(POOL_WINDOWS) - 1
SUBLANES = 8
HALO = SUBLANES * len(POOL_WINDOWS)
HEAD_DIM = 128
HEADS = D_MODEL // HEAD_DIM
LN_EPS = 1e-5
DN_ALPHA = (2 * DEPTH) ** 0.25
LOG2E = math.log2(math.e)
Q_SCALE = HEAD_DIM ** -0.5 * LOG2E
NEG_BIG = -1e30
SKIP_LOG2 = 150.0
LANES = 128
VMEM_LIMIT = 60 * 1024 * 1024


def _params(semantics):
    return pltpu.CompilerParams(dimension_semantics=semantics, vmem_limit_bytes=VMEM_LIMIT)


def _layer_norm(y, g, b):
    mu = jnp.mean(y, axis=-1, keepdims=True)
    yc = y - mu
    var = jnp.mean(yc * yc, axis=-1, keepdims=True)
    return yc * lax.rsqrt(var + LN_EPS) * g + b


def _ffn_kernel(x_ref, w1_ref, w3_ref, w2_ref, g_ref, b_ref, o_ref, *rest):
    *wb_refs, xb_ref = rest
    j = pl.program_id(1)

    @pl.when(j == 0)
    def _():
        xb_ref[...] = x_ref[...].astype(BF16)
        o_ref[...] = jnp.zeros_like(o_ref)

    w1, w3, w2 = w1_ref[...].astype(BF16), w3_ref[...].astype(BF16), w2_ref[...].astype(BF16)
    for wb_ref, w in zip(wb_refs, (w1, w3, w2)):
        wb_ref[...] = w
    xb = xb_ref[...]
    h1 = jnp.dot(xb, w1, preferred_element_type=F32)
    h3 = jnp.dot(xb, w3, preferred_element_type=F32)
    gate = (h1 * jax.nn.sigmoid(h1) * h3).astype(BF16)
    o_ref[...] += jnp.dot(gate, w2, preferred_element_type=F32)

    @pl.when(j == pl.num_programs(1) - 1)
    def _():
        y = DN_ALPHA * x_ref[...] + 0.5 * o_ref[...]
        o_ref[...] = _layer_norm(y, g_ref[...], b_ref[...])


def _ffn_ln(x, w1, w3, w2, g, b, *, tm, tf, layer_half=None):
    t, d = x.shape
    f = w1.shape[-1]
    emit = layer_half is not None
    assert not emit or t == tm

    def wspec(block, idx):
        if emit:
            return pl.BlockSpec((None, None) + block, lambda i, j: layer_half + idx(j))
        return pl.BlockSpec(block, lambda i, j: idx(j))

    up, down = wspec((d, tf), lambda j: (0, j)), wspec((tf, d), lambda j: (j, 0))
    out_specs = [pl.BlockSpec((tm, d), lambda i, j: (i, 0))]
    out_shape = [jax.ShapeDtypeStruct((t, d), F32)]
    if emit:
        out_specs += [pl.BlockSpec((d, tf), lambda i, j: (0, j)), pl.BlockSpec((d, tf), lambda i, j: (0, j)),
                      pl.BlockSpec((tf, d), lambda i, j: (j, 0))]
        out_shape += [jax.ShapeDtypeStruct((d, f), BF16), jax.ShapeDtypeStruct((d, f), BF16),
                      jax.ShapeDtypeStruct((f, d), BF16)]
    res = pl.pallas_call(
        _ffn_kernel,
        grid=(t // tm, f // tf),
        in_specs=[
            pl.BlockSpec((tm, d), lambda i, j: (i, 0)),
            up, up, down,
            pl.BlockSpec((1, d), lambda i, j: (0, 0)),
            pl.BlockSpec((1, d), lambda i, j: (0, 0)),
        ],
        out_specs=out_specs,
        out_shape=out_shape,
        scratch_shapes=[pltpu.VMEM((tm, d), BF16)],
        compiler_params=_params(("parallel", "arbitrary")),
        name="ffn_ln",
    )(x, w1, w3, w2, g, b)
    return res if emit else res[0]


def _pool_kernel(x_ref, halo_ref, pw_ref, ps_ref, g_ref, b_ref, o_ref, buf_ref, *, tm, start_pos, zero_first):
    i = pl.program_id(1)
    bb = x_ref.shape[0]
    total = HALO + tm
    x = x_ref[...]
    halo = halo_ref[...]
    if zero_first:
        halo = jnp.where(i == 0, 0.0, halo)
    pos = start_pos + i * tm + lax.broadcasted_iota(jnp.int32, (1, tm, 1), 1)
    ys = []
    for g, w in enumerate(POOL_WINDOWS):
        cols = slice(g * POOL_GROUP, (g + 1) * POOL_GROUP)
        xg = x[:, :, cols]
        buf_ref[0, :, 0:HALO, :] = halo[:, :, cols]
        buf_ref[0, :, HALO:, :] = xg
        src = 0
        for s in range(g + 1):
            lo, back = SUBLANES * (s + 1), 1 << s
            level = buf_ref[src, :, lo:total, :] + buf_ref[src, :, lo - back:total - back, :]
            if s < g:
                buf_ref[1 - src, :, lo:total, :] = level
                src = 1 - src
        win = level[:, HALO - lo:, :]
        cnt = jnp.minimum(pos + 1, w).astype(F32)
        diff = (win / cnt - xg).reshape(bb * tm, POOL_GROUP)
        ys.append(jnp.dot(diff.astype(BF16), pw_ref[g], preferred_element_type=F32))
    y = jnp.concatenate(ys, axis=-1) * ps_ref[...]
    out = _layer_norm(DN_ALPHA * x.reshape(bb * tm, D_MODEL) + y, g_ref[...], b_ref[...])
    o_ref[...] = out.reshape(bb, tm, D_MODEL)


def _pool_ln(x, halo_src, halo_map, pw, ps, g, b, *, bb, tm, start_pos, zero_first):
    nb, t, d = x.shape
    kern = functools.partial(_pool_kernel, tm=tm, start_pos=start_pos, zero_first=zero_first)
    return pl.pallas_call(
        kern,
        grid=(nb // bb, t // tm),
        in_specs=[
            pl.BlockSpec((bb, tm, d), lambda bi, i: (bi, i, 0)),
            pl.BlockSpec((bb, HALO, d), halo_map),
            pl.BlockSpec((len(POOL_WINDOWS), POOL_GROUP, POOL_GROUP), lambda bi, i: (0, 0, 0)),
            pl.BlockSpec((1, d), lambda bi, i: (0, 0)),
            pl.BlockSpec((1, d), lambda bi, i: (0, 0)),
            pl.BlockSpec((1, d), lambda bi, i: (0, 0)),
        ],
        out_specs=pl.BlockSpec((bb, tm, d), lambda bi, i: (bi, i, 0)),
        out_shape=jax.ShapeDtypeStruct((nb, t, d), F32),
        scratch_shapes=[pltpu.VMEM((2, bb, HALO + tm, POOL_GROUP), F32)],
        compiler_params=_params(("parallel", "arbitrary")),
        name="pool_ln",
    )(x, halo_src, pw, ps, g, b)


def _max_sq_norm_per_head(xb, first_head, acc):
    sq = xb.astype(F32)
    sq = sq * sq
    lane = lax.broadcasted_iota(jnp.int32, acc.shape, 1)
    for hh in range(xb.shape[1] // HEAD_DIM):
        row = jnp.sum(sq[:, hh * HEAD_DIM:(hh + 1) * HEAD_DIM], axis=-1, keepdims=True)
        acc = jnp.where(lane == first_head + hh, jnp.max(row, axis=0, keepdims=True), acc)
    return acc


def _proj_kernel(x_ref, wq_ref, wk_ref, wv_ref, wf_ref, bf_ref,
                 q_ref, k_ref, v_ref, kb_ref, vb_ref, lf_ref, qn_ref, kn_ref, xb_ref):
    j = pl.program_id(1)
    heads_per_step = q_ref.shape[1] // HEAD_DIM

    @pl.when(j == 0)
    def _():
        xb = x_ref[...].astype(BF16)
        xb_ref[...] = xb
        fl = jnp.dot(xb, wf_ref[...], preferred_element_type=F32) + bf_ref[...]
        lf_ref[...] = jnp.minimum(fl, 0.0) - jnp.log1p(jnp.exp(-jnp.abs(fl)))
        qn_ref[...] = jnp.zeros_like(qn_ref)
        kn_ref[...] = jnp.zeros_like(kn_ref)

    xb = xb_ref[...]
    qb = (jnp.dot(xb, wq_ref[...], preferred_element_type=F32) * Q_SCALE).astype(BF16)
    q_ref[...] = qb
    k = jnp.dot(xb, wk_ref[...], preferred_element_type=F32)
    kb = k.astype(BF16)
    kb_ref[...] = kb
    v = jnp.dot(xb, wv_ref[...], preferred_element_type=F32)
    vb_ref[...] = v.astype(BF16)
    tm = x_ref.shape[0]
    for hh in range(heads_per_step):
        rows = pl.ds(j * heads_per_step + hh, tm, stride=HEADS)
        k_ref[rows, :] = k[:, hh * HEAD_DIM:(hh + 1) * HEAD_DIM]
        v_ref[rows, :] = v[:, hh * HEAD_DIM:(hh + 1) * HEAD_DIM]
    qn_ref[...] = _max_sq_norm_per_head(qb, j * heads_per_step, qn_ref[...])
    kn_ref[...] = _max_sq_norm_per_head(kb, j * heads_per_step, kn_ref[...])


def _fox_proj(x, w_in, wf, bf, *, tm=512, tn=512):
    t, d = x.shape
    nd = d // tn
    row = pl.BlockSpec((tm, tn), lambda i, j: (i, j))

    def wcols(part):
        return pl.BlockSpec((d, tn), lambda i, j: (0, part * nd + j))

    wq, wk, wv = wcols(0), wcols(1), wcols(2)
    tile_stat = pl.BlockSpec((None, SUBLANES, LANES), lambda i, j: (i, 0, 0))
    by_head = pl.BlockSpec((tm * HEADS, HEAD_DIM), lambda i, j: (i, 0))
    return pl.pallas_call(
        _proj_kernel,
        grid=(t // tm, d // tn),
        in_specs=[
            pl.BlockSpec((tm, d), lambda i, j: (i, 0)),
            wq, wk, wv,
            pl.BlockSpec((d, LANES), lambda i, j: (0, 0)),
            pl.BlockSpec((1, LANES), lambda i, j: (0, 0)),
        ],
        out_specs=[row, by_head, by_head, row, row, pl.BlockSpec((tm, LANES), lambda i, j: (i, 0)), tile_stat, tile_stat],
        out_shape=[
            jax.ShapeDtypeStruct((t, d), BF16),
            jax.ShapeDtypeStruct((t * HEADS, HEAD_DIM), F32),
            jax.ShapeDtypeStruct((t * HEADS, HEAD_DIM), F32),
            jax.ShapeDtypeStruct((t, d), BF16),
            jax.ShapeDtypeStruct((t, d), BF16),
            jax.ShapeDtypeStruct((t, LANES), F32),
            jax.ShapeDtypeStruct((t // tm, SUBLANES, LANES), F32),
            jax.ShapeDtypeStruct((t // tm, SUBLANES, LANES), F32),
        ],
        scratch_shapes=[pltpu.VMEM((tm, d), BF16)],
        compiler_params=_params(("parallel", "arbitrary")),
        name="fox_proj",
    )(x, w_in, w_in, w_in, wf, bf)


def _split3(x):
    hi = x.astype(BF16)
    rem = x - hi.astype(F32)
    mid = rem.astype(BF16)
    lo = (rem - mid.astype(F32)).astype(BF16)
    return hi, mid, lo


def _cumsum_kernel(x_ref, o_ref, carry_ref, *, tc):
    @pl.when(pl.program_id(1) == 0)
    def _():
        carry_ref[...] = jnp.zeros_like(carry_ref)

    r = lax.broadcasted_iota(jnp.int32, (tc, tc), 0)
    c = lax.broadcasted_iota(jnp.int32, (tc, tc), 1)
    tri = (c <= r).astype(BF16)
    hi, mid, lo = _split3(x_ref[...])
    cs = (jnp.dot(tri, hi, preferred_element_type=F32)
          + jnp.dot(tri, mid, preferred_element_type=F32)
          + jnp.dot(tri, lo, preferred_element_type=F32)) + carry_ref[...]
    o_ref[...] = cs * LOG2E
    carry_ref[...] = cs[tc - 1:tc, :]


def _cumsum_time(x, *, tc):
    nb, t, w = x.shape
    return pl.pallas_call(
        functools.partial(_cumsum_kernel, tc=tc),
        grid=(nb, t // tc),
        in_specs=[pl.BlockSpec((None, tc, w), lambda b, j: (b, j, 0))],
        out_specs=pl.BlockSpec((None, tc, w), lambda b, j: (b, j, 0)),
        out_shape=jax.ShapeDtypeStruct((nb, t, w), F32),
        scratch_shapes=[pltpu.VMEM((1, w), F32)],
        compiler_params=_params(("parallel", "arbitrary")),
        name="cumsum_time",
    )(x)


N_SPLIT = 3
GROUP = LANES // HEADS
assert 2 * N_SPLIT <= GROUP
QK_AHEAD = 1


def _bias_selectors():
    sel_q = np.zeros((N_SPLIT * LANES, LANES), np.float32)
    sel_k = np.zeros((N_SPLIT * LANES, LANES), np.float32)
    one_q = np.zeros((1, LANES), np.float32)
    one_k = np.zeros((1, LANES), np.float32)
    for h in range(HEADS):
        for part in range(N_SPLIT):
            sel_q[part * LANES + h, h * GROUP + part] = 1.0
            sel_k[part * LANES + h, h * GROUP + N_SPLIT + part] = -1.0
            one_q[0, h * GROUP + N_SPLIT + part] = 1.0
            one_k[0, h * GROUP + part] = 1.0
    return jnp.asarray(sel_q, BF16), jnp.asarray(sel_k, BF16), jnp.asarray(one_q), jnp.asarray(one_k)


def _bias_operands_kernel(c_ref, sel_q_ref, sel_k_ref, one_q_ref, one_k_ref, a_ref, b_ref):
    parts = jnp.concatenate(_split3(c_ref[...]), axis=1)
    a_ref[...] = (jnp.dot(parts, sel_q_ref[...], preferred_element_type=F32) + one_q_ref[...]).astype(BF16)
    b_ref[...] = (jnp.dot(parts, sel_k_ref[...], preferred_element_type=F32) + one_k_ref[...]).astype(BF16)


def _bias_operands(c2, *, tr=2048):
    s, w = c2.shape
    rows = pl.BlockSpec((tr, w), lambda i: (i, 0))
    sel = pl.BlockSpec((N_SPLIT * w, w), lambda i: (0, 0))
    one = pl.BlockSpec((1, w), lambda i: (0, 0))
    return pl.pallas_call(
        _bias_operands_kernel,
        grid=(s // tr,),
        in_specs=[rows, sel, sel, one, one],
        out_specs=[rows, rows],
        out_shape=[jax.ShapeDtypeStruct((s, w), BF16), jax.ShapeDtypeStruct((s, w), BF16)],
        compiler_params=_params(("parallel",)),
        name="bias_operands",
    )(c2, *_bias_selectors())


def _qk(q_h, k_h):
    return lax.dot_general(q_h, k_h, (((1,), (1,)), ((), ())), preferred_element_type=F32)


def _attn_schedule(qmax, kmax, c2, tq, nsteps):
    nq = qmax.shape[0]
    qn, kn = jnp.sqrt(qmax), jnp.sqrt(kmax)
    c_first, c_last = c2[0::tq, :HEADS], c2[tq - 1::tq, :HEADS]
    bound = qn[:, None, :] * (kn[None, :, :] + kn[:, None, :]) + c_first[:, None, :] - c_last[None, :, :]
    tile = jnp.arange(nq, dtype=jnp.int32)
    skip = jnp.all(bound < -SKIP_LOG2, axis=-1) & (tile[None, :] < tile[:, None])
    jstart = jnp.argmin(skip.astype(jnp.int32), axis=1).astype(jnp.int32)
    cnt = tile - jstart + 1
    ends = jnp.cumsum(cnt)
    starts = ends - cnt
    n = jnp.arange(nsteps, dtype=jnp.int32)
    valid = n < ends[-1]
    qi = jnp.minimum(jnp.sum((ends[None, :] <= n[:, None]).astype(jnp.int32), axis=1), nq - 1)
    kj = jstart[qi] + n - starts[qi]
    qi = jnp.where(valid, qi, nq - 1)
    kj = jnp.where(valid, kj, nq - 1)
    first = valid & (kj == jstart[qi])
    return qi, kj, valid.astype(jnp.int32) + 2 * first.astype(jnp.int32)


def _attn_prompt_kernel(qi_ref, kj_ref, fl_ref, q_ref, qc_ref, k_ref, kc_ref, v_ref, o_ref,
                        m_ref, l_ref, acc_ref, qch_ref, *, tq):
    n = pl.program_id(0)
    flags = fl_ref[n]
    nchunk = tq // LANES

    @pl.when(flags >= 2)
    def _():
        m_ref[...] = jnp.full_like(m_ref, NEG_BIG)
        l_ref[...] = jnp.zeros_like(l_ref)
        acc_ref[...] = jnp.zeros_like(acc_ref)
        qc = qc_ref[...]
        group = lax.broadcasted_iota(jnp.int32, qc.shape, 1) // GROUP
        for h in range(HEADS):
            qch_ref[h] = jnp.where(group == h, qc, jnp.zeros_like(qc))

    def sweep(masked):
        if masked:
            row = lax.broadcasted_iota(jnp.int32, (tq, LANES), 0)
            lane = lax.broadcasted_iota(jnp.int32, (tq, LANES), 1)

        def logits(h):
            sl = slice(h * HEAD_DIM, (h + 1) * HEAD_DIM)
            return _qk(jnp.concatenate([q_ref[:, sl], qch_ref[h]], axis=1),
                       jnp.concatenate([k_ref[:, sl], kc_ref[...]], axis=1))

        pending = [logits(h) for h in range(QK_AHEAD)]
        for h in range(HEADS):
            sl = slice(h * HEAD_DIM, (h + 1) * HEAD_DIM)
            if h + QK_AHEAD < HEADS:
                pending.append(logits(h + QK_AHEAD))
            s = pending.pop(0)
            chunks = []
            for c in range(nchunk):
                sc = s[:, c * LANES:(c + 1) * LANES]
                if masked:
                    sc = jnp.where(lane + c * LANES <= row, sc, NEG_BIG)
                chunks.append(sc)
            mx = chunks[0]
            for sc in chunks[1:]:
                mx = jnp.maximum(mx, sc)
            m_prev = m_ref[h]
            m_new = jnp.maximum(m_prev, jnp.max(mx, axis=-1, keepdims=True))
            a = jnp.exp2(m_prev - m_new)
            ps = [jnp.exp2(sc - m_new) for sc in chunks]
            lsum = ps[0]
            for p in ps[1:]:
                lsum = lsum + p
            l_ref[h] = a * l_ref[h] + lsum
            p = jnp.concatenate(ps, axis=-1).astype(BF16)
            acc_ref[:, sl] = a * acc_ref[:, sl] + jnp.dot(p, v_ref[:, sl], preferred_element_type=F32)
            m_ref[h] = m_new

    diagonal = kj_ref[n] == qi_ref[n]

    @pl.when((flags >= 1) & jnp.logical_not(diagonal))
    def _():
        sweep(False)

    @pl.when((flags >= 1) & diagonal)
    def _():
        sweep(True)
        for h in range(HEADS):
            sl = slice(h * HEAD_DIM, (h + 1) * HEAD_DIM)
            l_tot = jnp.sum(l_ref[h], axis=-1, keepdims=True)
            o_ref[:, sl] = (acc_ref[:, sl] / l_tot).astype(BF16)


def _attn_prompt(q, kb, vb, c2, qmax, kmax, *, tq=512):
    s, d = q.shape
    nq = s // tq
    nsteps = nq * (nq + 1) // 2
    qi, kj, flags = _attn_schedule(qmax, kmax, c2, tq, nsteps)
    qc, kc = _bias_operands(c2)
    qrow = pl.BlockSpec((tq, d), lambda n, qi, kj, fl: (qi[n], 0))
    krow = pl.BlockSpec((tq, d), lambda n, qi, kj, fl: (kj[n], 0))
    qext = pl.BlockSpec((tq, LANES), lambda n, qi, kj, fl: (qi[n], 0))
    kext = pl.BlockSpec((tq, LANES), lambda n, qi, kj, fl: (kj[n], 0))
    stat = pltpu.VMEM((HEADS, tq, LANES), F32)
    return pl.pallas_call(
        functools.partial(_attn_prompt_kernel, tq=tq),
        grid_spec=pltpu.PrefetchScalarGridSpec(
            num_scalar_prefetch=3,
            grid=(nsteps,),
            in_specs=[qrow, qext, krow, kext, krow],
            out_specs=qrow,
            scratch_shapes=[stat, stat, pltpu.VMEM((tq, d), F32), pltpu.VMEM((HEADS, tq, LANES), BF16)],
        ),
        out_shape=jax.ShapeDtypeStruct((s, d), BF16),
        compiler_params=_params(("arbitrary",)),
        name="attn_prompt",
    )(qi, kj, flags, q, qc, kb, kc, vb)


def _attn_sample_kernel(q_ref, kc_ref, vc_ref, kn_ref, vn_ref, cq_ref, ctc_ref, ctn_ref, o_ref,
                        m_ref, l_ref, acc_ref, cqr_ref, *, tp):
    j = pl.program_id(1)
    tnew = q_ref.shape[0]
    heads = [slice(h * HEAD_DIM, (h + 1) * HEAD_DIM) for h in range(HEADS)]

    @pl.when(j == 0)
    def _():
        m_ref[...] = jnp.full_like(m_ref, NEG_BIG)
        l_ref[...] = jnp.zeros_like(l_ref)
        acc_ref[...] = jnp.zeros_like(acc_ref)
        for h in range(HEADS):
            cqr_ref[h * tnew:(h + 1) * tnew, :] = jnp.broadcast_to(cq_ref[:, h:h + 1], (tnew, LANES))

    def update(keys, values, key_c2, visible):
        n = key_c2.shape[1]
        s = jnp.concatenate([_qk(q_ref[:, heads[h]], keys[h]) for h in range(HEADS)], axis=0)
        c_keys = jnp.concatenate([jnp.broadcast_to(key_c2[h:h + 1, :], (tnew, n)) for h in range(HEADS)], axis=0)
        s = s + cqr_ref[:, :1] - c_keys
        if visible is not None:
            s = jnp.where(visible, s, NEG_BIG)
        m_prev = m_ref[...]
        m_new = jnp.maximum(m_prev, jnp.max(s, axis=-1, keepdims=True))
        a = jnp.exp2(m_prev - m_new)
        p = jnp.exp2(s - m_new[:, :1])
        l_ref[...] = a * l_ref[...] + jnp.sum(p, axis=-1, keepdims=True)
        pb = p.astype(BF16)
        pv = jnp.concatenate([jnp.dot(pb[h * tnew:(h + 1) * tnew], values[h], preferred_element_type=F32)
                              for h in range(HEADS)], axis=0)
        acc_ref[...] = a * acc_ref[...] + pv
        m_ref[...] = m_new

    update([kc_ref[pl.ds(h, tp, stride=HEADS), :].astype(BF16) for h in range(HEADS)],
           [vc_ref[pl.ds(h, tp, stride=HEADS), :].astype(BF16) for h in range(HEADS)],
           ctc_ref[...], None)

    @pl.when(j == pl.num_programs(1) - 1)
    def _():
        query = lax.broadcasted_iota(jnp.int32, (HEADS * tnew, tnew), 0) % tnew
        key = lax.broadcasted_iota(jnp.int32, (HEADS * tnew, tnew), 1)
        update([kn_ref[:, sl] for sl in heads], [vn_ref[:, sl] for sl in heads], ctn_ref[...], key <= query)
        out = acc_ref[...] / l_ref[...]
        for h in range(HEADS):
            o_ref[:, heads[h]] = out[h * tnew:(h + 1) * tnew, :].astype(BF16)


def _attn_sample(q, k_cache, v_cache, kb_new, vb_new, c_new, ct_cache, ct_new, *, tnew, tp=512):
    t, d = q.shape
    nb, past_rows, _ = k_cache.shape
    new_rows = pl.BlockSpec((tnew, d), lambda b, j: (b, 0))
    cache = pl.BlockSpec((None, tp * HEADS, HEAD_DIM), lambda b, j: (b, j, 0))
    return pl.pallas_call(
        functools.partial(_attn_sample_kernel, tp=tp),
        grid=(nb, past_rows // (tp * HEADS)),
        in_specs=[
            new_rows, cache, cache, new_rows, new_rows,
            pl.BlockSpec((tnew, LANES), lambda b, j: (b, 0)),
            pl.BlockSpec((None, HEADS, tp), lambda b, j: (b, 0, j)),
            pl.BlockSpec((None, HEADS, tnew), lambda b, j: (b, 0, 0)),
        ],
        out_specs=new_rows,
        out_shape=jax.ShapeDtypeStruct((t, d), BF16),
        scratch_shapes=[pltpu.VMEM((HEADS * tnew, LANES), F32), pltpu.VMEM((HEADS * tnew, LANES), F32),
                        pltpu.VMEM((HEADS * tnew, HEAD_DIM), F32), pltpu.VMEM((HEADS * tnew, LANES), F32)],
        compiler_params=_params(("parallel", "arbitrary")),
        name="attn_sample",
    )(q, k_cache, v_cache, kb_new, vb_new, c_new, ct_cache, ct_new)


def _oproj_kernel(x_ref, o_ref, wo_ref, g_ref, b_ref, out_ref):
    y = DN_ALPHA * x_ref[...] + jnp.dot(o_ref[...], wo_ref[...], preferred_element_type=F32)
    out_ref[...] = _layer_norm(y, g_ref[...], b_ref[...])


def _oproj_ln(x, o, wo, g, b, *, tm=512):
    t, d = x.shape
    row = pl.BlockSpec((tm, d), lambda i: (i, 0))
    vec = pl.BlockSpec((1, d), lambda i: (0, 0))
    return pl.pallas_call(
        _oproj_kernel,
        grid=(t // tm,),
        in_specs=[row, row, pl.BlockSpec((d, d), lambda i: (0, 0)), vec, vec],
        out_specs=row,
        out_shape=jax.ShapeDtypeStruct((t, d), F32),
        compiler_params=_params(("parallel",)),
        name="oproj_ln",
    )(x, o, wo, g, b)


def kernel(x_prompt, x_sample, state_pool, cache_fox_k, cache_fox_v, cache_fox_logf, ln_g, ln_b,
           ffn_w1, ffn_w3, ffn_w2, pool_w, pool_scale, fox_w_in, fox_b_f, fox_w_o):
    d = D_MODEL
    _, seq, _ = x_prompt.shape
    nb, tnew, _ = x_sample.shape
    past = cache_fox_k.shape[2]
    ns = nb * tnew

    def ffn(xp, xs, i, s, ln_idx):
        g, b = ln_g[i, ln_idx][None], ln_b[i, ln_idx][None]
        xs, w1b, w3b, w2b = _ffn_ln(xs, ffn_w1, ffn_w3, ffn_w2, g, b, tm=ns, tf=256, layer_half=(i, s))
        return _ffn_ln(xp, w1b, w3b, w2b, g, b, tm=512, tf=1024), xs

    xp, xs = x_prompt.reshape(seq, d), x_sample.reshape(ns, d)

    xp, xs = ffn(xp, xs, 0, 0, 0)
    xp3, xs3 = xp.reshape(1, seq, d), xs.reshape(nb, tnew, d)
    pool_prompt = xp3[:, seq - POOL_STATE:][None]
    pool_sample = jnp.concatenate([state_pool[0], xs3], axis=1)[:, -POOL_STATE:][None]
    pw = pool_w[0].astype(BF16)
    ps, g1, b1 = pool_scale[0][None], ln_g[0, 1][None], ln_b[0, 1][None]
    tm = 512
    xp = _pool_ln(xp3, xp3, lambda bi, i: (bi, jnp.maximum(i * (tm // HALO) - 1, 0), 0), pw, ps, g1, b1,
                  bb=1, tm=tm, start_pos=0, zero_first=True).reshape(seq, d)
    prev = jnp.pad(state_pool[0], ((0, 0), (HALO - POOL_STATE, 0), (0, 0)))
    xs = _pool_ln(xs3, prev, lambda bi, i: (bi, 0, 0), pw, ps, g1, b1,
                  bb=nb, tm=tnew, start_pos=past, zero_first=False).reshape(ns, d)
    xp, xs = ffn(xp, xs, 0, 1, 2)

    xp, xs = ffn(xp, xs, 1, 0, 0)
    w_in = fox_w_in[0].astype(BF16)
    wf = jnp.pad(w_in[:, 3 * d:], ((0, 0), (0, LANES - HEADS)))
    bf = jnp.pad(fox_b_f[0], (0, LANES - HEADS))[None]
    q_p, k_p, v_p, kb_p, vb_p, lf_p, qmax, kmax = _fox_proj(xp, w_in, wf, bf)
    q_s, k_s, v_s, kb_s, vb_s, lf_s, _, _ = _fox_proj(xs, w_in, wf, bf)

    c_p = _cumsum_time(lf_p[None], tc=512)[0]
    o_p = _attn_prompt(q_p, kb_p, vb_p, c_p, qmax[:, 0, :HEADS], kmax[:, 0, :HEADS])

    total = past + tnew
    lf_all = jnp.concatenate([cache_fox_logf[0], lf_s[:, :HEADS].reshape(nb, tnew, HEADS)], axis=1)
    c_s = _cumsum_time(lf_all.transpose(1, 0, 2).reshape(1, total, nb * HEADS), tc=total // 3)
    c_s = c_s.reshape(total, nb, HEADS)
    ct_s = c_s.transpose(1, 2, 0)
    c_new = jnp.pad(c_s[past:].transpose(1, 0, 2).reshape(ns, HEADS), ((0, 0), (0, LANES - HEADS)))
    o_s = _attn_sample(q_s, cache_fox_k[0].reshape(nb, past * HEADS, HEAD_DIM),
                       cache_fox_v[0].reshape(nb, past * HEADS, HEAD_DIM), kb_s, vb_s,
                       c_new, ct_s[:, :, :past], ct_s[:, :, past:], tnew=tnew)

    wo = fox_w_o[0].astype(BF16)
    g1, b1 = ln_g[1, 1][None], ln_b[1, 1][None]
    xp, xs = _oproj_ln(xp, o_p, wo, g1, b1), _oproj_ln(xs, o_s, wo, g1, b1)
    xp, xs = ffn(xp, xs, 1, 1, 2)

    shp = (HEADS, HEAD_DIM)
    return (xp.reshape(1, seq, d), xs.reshape(nb, tnew, d), pool_prompt, pool_sample,
            k_p.reshape(1, 1, seq, *shp), v_p.reshape(1, 1, seq, *shp), lf_p[:, :HEADS].reshape(1, 1, seq, HEADS),
            k_s.reshape(1, nb, tnew, *shp), v_s.reshape(1, nb, tnew, *shp),
            lf_s[:, :HEADS].reshape(1, nb, tnew, HEADS))
```

```python
import functools
import math

import jax
import jax.numpy as jnp
import numpy as np
from jax import lax
from jax.experimental import pallas as pl
from jax.experimental.pallas import tpu as pltpu

F32 = jnp.float32
BF16 = jnp.bfloat16

D_MODEL = 2048
DEPTH = 2
POOL_WINDOWS = (2, 4, 8, 16)
POOL_GROUP = D_MODEL // len(POOL_WINDOWS)
POOL_STATE = max(POOL_WINDOWS) - 1
SUBLANES = 8
HALO = SUBLANES * len(POOL_WINDOWS)
HEAD_DIM = 128
HEADS = D_MODEL // HEAD_DIM
LN_EPS = 1e-5
DN_ALPHA = (2 * DEPTH) ** 0.25
LOG2E = math.log2(math.e)
Q_SCALE = HEAD_DIM ** -0.5 * LOG2E
NEG_BIG = -1e30
SKIP_LOG2 = 150.0
LANES = 128
VMEM_LIMIT = 60 * 1024 * 1024


def _params(semantics):
    return pltpu.CompilerParams(dimension_semantics=semantics, vmem_limit_bytes=VMEM_LIMIT)


def _layer_norm(y, g, b):
    mu = jnp.mean(y, axis=-1, keepdims=True)
    yc = y - mu
    var = jnp.mean(yc * yc, axis=-1, keepdims=True)
    return yc * lax.rsqrt(var + LN_EPS) * g + b


def _ffn_kernel(x_ref, w1_ref, w3_ref, w2_ref, g_ref, b_ref, o_ref, *rest):
    *wb_refs, xb_ref = rest
    j = pl.program_id(1)

    def partial_sum():
        w1, w3, w2 = w1_ref[...].astype(BF16), w3_ref[...].astype(BF16), w2_ref[...].astype(BF16)
        for wb_ref, w in zip(wb_refs, (w1, w3, w2)):
            wb_ref[...] = w
        xb = xb_ref[...]
        h1 = jnp.dot(xb, w1, preferred_element_type=F32)
        h3 = jnp.dot(xb, w3, preferred_element_type=F32)
        gate = (h1 * jax.nn.sigmoid(h1) * h3).astype(BF16)
        return jnp.dot(gate, w2, preferred_element_type=F32)

    @pl.when(j == 0)
    def _():
        xb_ref[...] = x_ref[...].astype(BF16)
        o_ref[...] = partial_sum()

    @pl.when(j > 0)
    def _():
        o_ref[...] += partial_sum()

    @pl.when(j == pl.num_programs(1) - 1)
    def _():
        y = DN_ALPHA * x_ref[...] + 0.5 * o_ref[...]
        o_ref[...] = _layer_norm(y, g_ref[...], b_ref[...])


def _ffn_ln(x, w1, w3, w2, g, b, *, tm, tf, layer_half=None):
    t, d = x.shape
    f = w1.shape[-1]
    emit = layer_half is not None
    assert not emit or t == tm

    def wspec(block, idx):
        if emit:
            return pl.BlockSpec((None, None) + block, lambda i, j: layer_half + idx(j))
        return pl.BlockSpec(block, lambda i, j: idx(j))

    up, down = wspec((d, tf), lambda j: (0, j)), wspec((tf, d), lambda j: (j, 0))
    out_specs = [pl.BlockSpec((tm, d), lambda i, j: (i, 0))]
    out_shape = [jax.ShapeDtypeStruct((t, d), F32)]
    if emit:
        out_specs += [pl.BlockSpec((d, tf), lambda i, j: (0, j)), pl.BlockSpec((d, tf), lambda i, j: (0, j)),
                      pl.BlockSpec((tf, d), lambda i, j: (j, 0))]
        out_shape += [jax.ShapeDtypeStruct((d, f), BF16), jax.ShapeDtypeStruct((d, f), BF16),
                      jax.ShapeDtypeStruct((f, d), BF16)]
    res = pl.pallas_call(
        _ffn_kernel,
        grid=(t // tm, f // tf),
        in_specs=[
            pl.BlockSpec((tm, d), lambda i, j: (i, 0)),
            up, up, down,
            pl.BlockSpec((1, d), lambda i, j: (0, 0)),
            pl.BlockSpec((1, d), lambda i, j: (0, 0)),
        ],
        out_specs=out_specs,
        out_shape=out_shape,
        scratch_shapes=[pltpu.VMEM((tm, d), BF16)],
        compiler_params=_params(("parallel", "arbitrary")),
        name="ffn_ln",
    )(x, w1, w3, w2, g, b)
    return res if emit else res[0]


def _pool_kernel(x_ref, halo_ref, pw_ref, ps_ref, g_ref, b_ref, o_ref, buf_ref, *, tm, start_pos, zero_first):
    i = pl.program_id(1)
    bb = x_ref.shape[0]
    total = HALO + tm
    x = x_ref[...]
    halo = halo_ref[...]
    if zero_first:
        halo = jnp.where(i == 0, 0.0, halo)
    pos = start_pos + i * tm + lax.broadcasted_iota(jnp.int32, (1, tm, 1), 1)
    ys = []
    for g, w in enumerate(POOL_WINDOWS):
        cols = slice(g * POOL_GROUP, (g + 1) * POOL_GROUP)
        xg = x[:, :, cols]
        buf_ref[0, :, 0:HALO, :] = halo[:, :, cols]
        buf_ref[0, :, HALO:, :] = xg
        src = 0
        for s in range(g + 1):
            lo, back = SUBLANES * (s + 1), 1 << s
            level = buf_ref[src, :, lo:total, :] + buf_ref[src, :, lo - back:total - back, :]
            if s < g:
                buf_ref[1 - src, :, lo:total, :] = level
                src = 1 - src
        win = level[:, HALO - lo:, :]
        cnt = jnp.minimum(pos + 1, w).astype(F32)
        diff = (win / cnt - xg).reshape(bb * tm, POOL_GROUP)
        ys.append(jnp.dot(diff.astype(BF16), pw_ref[g], preferred_element_type=F32))
    y = jnp.concatenate(ys, axis=-1) * ps_ref[...]
    out = _layer_norm(DN_ALPHA * x.reshape(bb * tm, D_MODEL) + y, g_ref[...], b_ref[...])
    o_ref[...] = out.reshape(bb, tm, D_MODEL)


def _pool_ln(x, halo_src, halo_map, pw, ps, g, b, *, bb, tm, start_pos, zero_first):
    nb, t, d = x.shape
    kern = functools.partial(_pool_kernel, tm=tm, start_pos=start_pos, zero_first=zero_first)
    return pl.pallas_call(
        kern,
        grid=(nb // bb, t // tm),
        in_specs=[
            pl.BlockSpec((bb, tm, d), lambda bi, i: (bi, i, 0)),
            pl.BlockSpec((bb, HALO, d), halo_map),
            pl.BlockSpec((len(POOL_WINDOWS), POOL_GROUP, POOL_GROUP), lambda bi, i: (0, 0, 0)),
            pl.BlockSpec((1, d), lambda bi, i: (0, 0)),
            pl.BlockSpec((1, d), lambda bi, i: (0, 0)),
            pl.BlockSpec((1, d), lambda bi, i: (0, 0)),
        ],
        out_specs=pl.BlockSpec((bb, tm, d), lambda bi, i: (bi, i, 0)),
        out_shape=jax.ShapeDtypeStruct((nb, t, d), F32),
        scratch_shapes=[pltpu.VMEM((2, bb, HALO + tm, POOL_GROUP), F32)],
        compiler_params=_params(("parallel", "arbitrary")),
        name="pool_ln",
    )(x, halo_src, pw, ps, g, b)


def _max_sq_norm_per_head(xb, first_head, acc):
    sq = xb.astype(F32)
    sq = sq * sq
    lane = lax.broadcasted_iota(jnp.int32, acc.shape, 1)
    for hh in range(xb.shape[1] // HEAD_DIM):
        row = jnp.sum(sq[:, hh * HEAD_DIM:(hh + 1) * HEAD_DIM], axis=-1, keepdims=True)
        acc = jnp.where(lane == first_head + hh, jnp.max(row, axis=0, keepdims=True), acc)
    return acc


def _proj_kernel(x_ref, wq_ref, wk_ref, wv_ref, wf_ref, bf_ref,
                 q_ref, k_ref, v_ref, kb_ref, vb_ref, lf_ref, qn_ref, kn_ref, xb_ref):
    j = pl.program_id(1)
    heads_per_step = q_ref.shape[1] // HEAD_DIM

    @pl.when(j == 0)
    def _():
        xb = x_ref[...].astype(BF16)
        xb_ref[...] = xb
        fl = jnp.dot(xb, wf_ref[...], preferred_element_type=F32) + bf_ref[...]
        lf_ref[...] = jnp.minimum(fl, 0.0) - jnp.log1p(jnp.exp(-jnp.abs(fl)))
        qn_ref[...] = jnp.zeros_like(qn_ref)
        kn_ref[...] = jnp.zeros_like(kn_ref)

    xb = xb_ref[...]
    qb = (jnp.dot(xb, wq_ref[...], preferred_element_type=F32) * Q_SCALE).astype(BF16)
    q_ref[...] = qb
    k = jnp.dot(xb, wk_ref[...], preferred_element_type=F32)
    kb = k.astype(BF16)
    kb_ref[...] = kb
    v = jnp.dot(xb, wv_ref[...], preferred_element_type=F32)
    vb_ref[...] = v.astype(BF16)
    tm = x_ref.shape[0]
    for hh in range(heads_per_step):
        rows = pl.ds(j * heads_per_step + hh, tm, stride=HEADS)
        k_ref[rows, :] = k[:, hh * HEAD_DIM:(hh + 1) * HEAD_DIM]
        v_ref[rows, :] = v[:, hh * HEAD_DIM:(hh + 1) * HEAD_DIM]
    qn_ref[...] = _max_sq_norm_per_head(qb, j * heads_per_step, qn_ref[...])
    kn_ref[...] = _max_sq_norm_per_head(kb, j * heads_per_step, kn_ref[...])


def _fox_proj(x, w_in, wf, bf, *, tm=512, tn=512):
    t, d = x.shape
    nd = d // tn
    row = pl.BlockSpec((tm, tn), lambda i, j: (i, j))

    def wcols(part):
        return pl.BlockSpec((d, tn), lambda i, j: (0, part * nd + j))

    wq, wk, wv = wcols(0), wcols(1), wcols(2)
    tile_stat = pl.BlockSpec((None, SUBLANES, LANES), lambda i, j: (i, 0, 0))
    by_head = pl.BlockSpec((tm * HEADS, HEAD_DIM), lambda i, j: (i, 0))
    return pl.pallas_call(
        _proj_kernel,
        grid=(t // tm, d // tn),
        in_specs=[
            pl.BlockSpec((tm, d), lambda i, j: (i, 0)),
            wq, wk, wv,
            pl.BlockSpec((d, LANES), lambda i, j: (0, 0)),
            pl.BlockSpec((1, LANES), lambda i, j: (0, 0)),
        ],
        out_specs=[row, by_head, by_head, row, row, pl.BlockSpec((tm, LANES), lambda i, j: (i, 0)), tile_stat, tile_stat],
        out_shape=[
            jax.ShapeDtypeStruct((t, d), BF16),
            jax.ShapeDtypeStruct((t * HEADS, HEAD_DIM), F32),
            jax.ShapeDtypeStruct((t * HEADS, HEAD_DIM), F32),
            jax.ShapeDtypeStruct((t, d), BF16),
            jax.ShapeDtypeStruct((t, d), BF16),
            jax.ShapeDtypeStruct((t, LANES), F32),
            jax.ShapeDtypeStruct((t // tm, SUBLANES, LANES), F32),
            jax.ShapeDtypeStruct((t // tm, SUBLANES, LANES), F32),
        ],
        scratch_shapes=[pltpu.VMEM((tm, d), BF16)],
        compiler_params=_params(("parallel", "arbitrary")),
        name="fox_proj",
    )(x, w_in, w_in, w_in, wf, bf)


def _split3(x):
    hi = x.astype(BF16)
    rem = x - hi.astype(F32)
    mid = rem.astype(BF16)
    lo = (rem - mid.astype(F32)).astype(BF16)
    return hi, mid, lo


def _cumsum_kernel(x_ref, o_ref, carry_ref, *, tc):
    @pl.when(pl.program_id(1) == 0)
    def _():
        carry_ref[...] = jnp.zeros_like(carry_ref)

    r = lax.broadcasted_iota(jnp.int32, (tc, tc), 0)
    c = lax.broadcasted_iota(jnp.int32, (tc, tc), 1)
    tri = (c <= r).astype(BF16)
    hi, mid, lo = _split3(x_ref[...])
    cs = (jnp.dot(tri, hi, preferred_element_type=F32)
          + jnp.dot(tri, mid, preferred_element_type=F32)
          + jnp.dot(tri, lo, preferred_element_type=F32)) + carry_ref[...]
    o_ref[...] = cs * LOG2E
    carry_ref[...] = cs[tc - 1:tc, :]


def _cumsum_time(x, *, tc):
    nb, t, w = x.shape
    return pl.pallas_call(
        functools.partial(_cumsum_kernel, tc=tc),
        grid=(nb, t // tc),
        in_specs=[pl.BlockSpec((None, tc, w), lambda b, j: (b, j, 0))],
        out_specs=pl.BlockSpec((None, tc, w), lambda b, j: (b, j, 0)),
        out_shape=jax.ShapeDtypeStruct((nb, t, w), F32),
        scratch_shapes=[pltpu.VMEM((1, w), F32)],
        compiler_params=_params(("parallel", "arbitrary")),
        name="cumsum_time",
    )(x)


N_SPLIT = 3
GROUP = LANES // HEADS
assert 2 * N_SPLIT <= GROUP
QK_AHEAD = 1


def _bias_selectors():
    sel_q = np.zeros((N_SPLIT * LANES, LANES), np.float32)
    sel_k = np.zeros((N_SPLIT * LANES, LANES), np.float32)
    one_q = np.zeros((1, LANES), np.float32)
    one_k = np.zeros((1, LANES), np.float32)
    for h in range(HEADS):
        for part in range(N_SPLIT):
            sel_q[part * LANES + h, h * GROUP + part] = 1.0
            sel_k[part * LANES + h, h * GROUP + N_SPLIT + part] = -1.0
            one_q[0, h * GROUP + N_SPLIT + part] = 1.0
            one_k[0, h * GROUP + part] = 1.0
    return jnp.asarray(sel_q, BF16), jnp.asarray(sel_k, BF16), jnp.asarray(one_q), jnp.asarray(one_k)


def _bias_operands_kernel(c_ref, sel_q_ref, sel_k_ref, one_q_ref, one_k_ref, a_ref, b_ref):
    parts = jnp.concatenate(_split3(c_ref[...]), axis=1)
    a_ref[...] = (jnp.dot(parts, sel_q_ref[...], preferred_element_type=F32) + one_q_ref[...]).astype(BF16)
    b_ref[...] = (jnp.dot(parts, sel_k_ref[...], preferred_element_type=F32) + one_k_ref[...]).astype(BF16)


def _bias_operands(c2, *, tr=2048):
    s, w = c2.shape
    rows = pl.BlockSpec((tr, w), lambda i: (i, 0))
    sel = pl.BlockSpec((N_SPLIT * w, w), lambda i: (0, 0))
    one = pl.BlockSpec((1, w), lambda i: (0, 0))
    return pl.pallas_call(
        _bias_operands_kernel,
        grid=(s // tr,),
        in_specs=[rows, sel, sel, one, one],
        out_specs=[rows, rows],
        out_shape=[jax.ShapeDtypeStruct((s, w), BF16), jax.ShapeDtypeStruct((s, w), BF16)],
        compiler_params=_params(("parallel",)),
        name="bias_operands",
    )(c2, *_bias_selectors())


def _qk(q_h, k_h):
    return lax.dot_general(q_h, k_h, (((1,), (1,)), ((), ())), preferred_element_type=F32)


def _attn_schedule(qmax, kmax, c2, tq, nsteps):
    nq = qmax.shape[0]
    qn, kn = jnp.sqrt(qmax), jnp.sqrt(kmax)
    c_first, c_last = c2[0::tq, :HEADS], c2[tq - 1::tq, :HEADS]
    bound = qn[:, None, :] * (kn[None, :, :] + kn[:, None, :]) + c_first[:, None, :] - c_last[None, :, :]
    tile = jnp.arange(nq, dtype=jnp.int32)
    skip = jnp.all(bound < -SKIP_LOG2, axis=-1) & (tile[None, :] < tile[:, None])
    jstart = jnp.argmin(skip.astype(jnp.int32), axis=1).astype(jnp.int32)
    cnt = tile - jstart + 1
    ends = jnp.cumsum(cnt)
    starts = ends - cnt
    n = jnp.arange(nsteps, dtype=jnp.int32)
    valid = n < ends[-1]
    qi = jnp.minimum(jnp.sum((ends[None, :] <= n[:, None]).astype(jnp.int32), axis=1), nq - 1)
    kj = jstart[qi] + n - starts[qi]
    qi = jnp.where(valid, qi, nq - 1)
    kj = jnp.where(valid, kj, nq - 1)
    first = valid & (kj == jstart[qi])
    return qi, kj, valid.astype(jnp.int32) + 2 * first.astype(jnp.int32)


def _attn_prompt_kernel(qi_ref, kj_ref, fl_ref, q_ref, qc_ref, k_ref, kc_ref, v_ref, o_ref,
                        m_ref, l_ref, acc_ref, qch_ref, *, tq):
    n = pl.program_id(0)
    flags = fl_ref[n]
    nchunk = tq // LANES

    @pl.when(flags >= 2)
    def _():
        m_ref[...] = jnp.full_like(m_ref, NEG_BIG)
        l_ref[...] = jnp.zeros_like(l_ref)
        acc_ref[...] = jnp.zeros_like(acc_ref)
        qc = qc_ref[...]
        group = lax.broadcasted_iota(jnp.int32, qc.shape, 1) // GROUP
        for h in range(HEADS):
            qch_ref[h] = jnp.where(group == h, qc, jnp.zeros_like(qc))

    def sweep(masked):
        if masked:
            row = lax.broadcasted_iota(jnp.int32, (tq, LANES), 0)
            lane = lax.broadcasted_iota(jnp.int32, (tq, LANES), 1)

        def logits(h):
            sl = slice(h * HEAD_DIM, (h + 1) * HEAD_DIM)
            return _qk(jnp.concatenate([q_ref[:, sl], qch_ref[h]], axis=1),
                       jnp.concatenate([k_ref[:, sl], kc_ref[...]], axis=1))

        pending = [logits(h) for h in range(QK_AHEAD)]
        for h in range(HEADS):
            sl = slice(h * HEAD_DIM, (h + 1) * HEAD_DIM)
            if h + QK_AHEAD < HEADS:
                pending.append(logits(h + QK_AHEAD))
            s = pending.pop(0)
            chunks = []
            for c in range(nchunk):
                sc = s[:, c * LANES:(c + 1) * LANES]
                if masked:
                    sc = jnp.where(lane + c * LANES <= row, sc, NEG_BIG)
                chunks.append(sc)
            mx = chunks[0]
            for sc in chunks[1:]:
                mx = jnp.maximum(mx, sc)
            m_prev = m_ref[h]
            m_new = jnp.maximum(m_prev, jnp.max(mx, axis=-1, keepdims=True))
            a = jnp.exp2(m_prev - m_new)
            ps = [jnp.exp2(sc - m_new) for sc in chunks]
            lsum = ps[0]
            for p in ps[1:]:
                lsum = lsum + p
            l_ref[h] = a * l_ref[h] + lsum
            p = jnp.concatenate(ps, axis=-1).astype(BF16)
            acc_ref[:, sl] = a * acc_ref[:, sl] + jnp.dot(p, v_ref[:, sl], preferred_element_type=F32)
            m_ref[h] = m_new

    diagonal = kj_ref[n] == qi_ref[n]

    @pl.when((flags >= 1) & jnp.logical_not(diagonal))
    def _():
        sweep(False)

    @pl.when((flags >= 1) & diagonal)
    def _():
        sweep(True)
        for h in range(HEADS):
            sl = slice(h * HEAD_DIM, (h + 1) * HEAD_DIM)
            l_tot = jnp.sum(l_ref[h], axis=-1, keepdims=True)
            o_ref[:, sl] = (acc_ref[:, sl] / l_tot).astype(BF16)


def _attn_prompt(q, kb, vb, c2, qmax, kmax, *, tq=512):
    s, d = q.shape
    nq = s // tq
    nsteps = nq * (nq + 1) // 2
    qi, kj, flags = _attn_schedule(qmax, kmax, c2, tq, nsteps)
    qc, kc = _bias_operands(c2)
    qrow = pl.BlockSpec((tq, d), lambda n, qi, kj, fl: (qi[n], 0))
    krow = pl.BlockSpec((tq, d), lambda n, qi, kj, fl: (kj[n], 0))
    qext = pl.BlockSpec((tq, LANES), lambda n, qi, kj, fl: (qi[n], 0))
    kext = pl.BlockSpec((tq, LANES), lambda n, qi, kj, fl: (kj[n], 0))
    stat = pltpu.VMEM((HEADS, tq, LANES), F32)
    return pl.pallas_call(
        functools.partial(_attn_prompt_kernel, tq=tq),
        grid_spec=pltpu.PrefetchScalarGridSpec(
            num_scalar_prefetch=3,
            grid=(nsteps,),
            in_specs=[qrow, qext, krow, kext, krow],
            out_specs=qrow,
            scratch_shapes=[stat, stat, pltpu.VMEM((tq, d), F32), pltpu.VMEM((HEADS, tq, LANES), BF16)],
        ),
        out_shape=jax.ShapeDtypeStruct((s, d), BF16),
        compiler_params=_params(("arbitrary",)),
        name="attn_prompt",
    )(qi, kj, flags, q, qc, kb, kc, vb)


def _attn_sample_kernel(q_ref, kc_ref, vc_ref, kn_ref, vn_ref, cq_ref, ctc_ref, ctn_ref, o_ref,
                        m_ref, l_ref, acc_ref, cqr_ref, *, tp):
    j = pl.program_id(1)
    tnew = q_ref.shape[0]
    heads = [slice(h * HEAD_DIM, (h + 1) * HEAD_DIM) for h in range(HEADS)]

    @pl.when(j == 0)
    def _():
        m_ref[...] = jnp.full_like(m_ref, NEG_BIG)
        l_ref[...] = jnp.zeros_like(l_ref)
        acc_ref[...] = jnp.zeros_like(acc_ref)
        for h in range(HEADS):
            cqr_ref[h * tnew:(h + 1) * tnew, :] = jnp.broadcast_to(cq_ref[:, h:h + 1], (tnew, LANES))

    def update(keys, values, key_c2, visible):
        n = key_c2.shape[1]
        s = jnp.concatenate([_qk(q_ref[:, heads[h]], keys[h]) for h in range(HEADS)], axis=0)
        c_keys = jnp.concatenate([jnp.broadcast_to(key_c2[h:h + 1, :], (tnew, n)) for h in range(HEADS)], axis=0)
        s = s + cqr_ref[:, :1] - c_keys
        if visible is not None:
            s = jnp.where(visible, s, NEG_BIG)
        m_prev = m_ref[...]
        m_new = jnp.maximum(m_prev, jnp.max(s, axis=-1, keepdims=True))
        a = jnp.exp2(m_prev - m_new)
        p = jnp.exp2(s - m_new[:, :1])
        l_ref[...] = a * l_ref[...] + jnp.sum(p, axis=-1, keepdims=True)
        pb = p.astype(BF16)
        pv = jnp.concatenate([jnp.dot(pb[h * tnew:(h + 1) * tnew], values[h], preferred_element_type=F32)
                              for h in range(HEADS)], axis=0)
        acc_ref[...] = a * acc_ref[...] + pv
        m_ref[...] = m_new

    update([kc_ref[pl.ds(h, tp, stride=HEADS), :].astype(BF16) for h in range(HEADS)],
           [vc_ref[pl.ds(h, tp, stride=HEADS), :].astype(BF16) for h in range(HEADS)],
           ctc_ref[...], None)

    @pl.when(j == pl.num_programs(1) - 1)
    def _():
        query = lax.broadcasted_iota(jnp.int32, (HEADS * tnew, tnew), 0) % tnew
        key = lax.broadcasted_iota(jnp.int32, (HEADS * tnew, tnew), 1)
        update([kn_ref[:, sl] for sl in heads], [vn_ref[:, sl] for sl in heads], ctn_ref[...], key <= query)
        out = acc_ref[...] / l_ref[...]
        for h in range(HEADS):
            o_ref[:, heads[h]] = out[h * tnew:(h + 1) * tnew, :].astype(BF16)


def _attn_sample(q, k_cache, v_cache, kb_new, vb_new, c_new, ct_cache, ct_new, *, tnew, tp=512):
    t, d = q.shape
    nb, past_rows, _ = k_cache.shape
    new_rows = pl.BlockSpec((tnew, d), lambda b, j: (b, 0))
    cache = pl.BlockSpec((None, tp * HEADS, HEAD_DIM), lambda b, j: (b, j, 0))
    return pl.pallas_call(
        functools.partial(_attn_sample_kernel, tp=tp),
        grid=(nb, past_rows // (tp * HEADS)),
        in_specs=[
            new_rows, cache, cache, new_rows, new_rows,
            pl.BlockSpec((tnew, LANES), lambda b, j: (b, 0)),
            pl.BlockSpec((None, HEADS, tp), lambda b, j: (b, 0, j)),
            pl.BlockSpec((None, HEADS, tnew), lambda b, j: (b, 0, 0)),
        ],
        out_specs=new_rows,
        out_shape=jax.ShapeDtypeStruct((t, d), BF16),
        scratch_shapes=[pltpu.VMEM((HEADS * tnew, LANES), F32), pltpu.VMEM((HEADS * tnew, LANES), F32),
                        pltpu.VMEM((HEADS * tnew, HEAD_DIM), F32), pltpu.VMEM((HEADS * tnew, LANES), F32)],
        compiler_params=_params(("parallel", "arbitrary")),
        name="attn_sample",
    )(q, k_cache, v_cache, kb_new, vb_new, c_new, ct_cache, ct_new)


def _oproj_kernel(x_ref, o_ref, wo_ref, g_ref, b_ref, out_ref):
    y = DN_ALPHA * x_ref[...] + jnp.dot(o_ref[...], wo_ref[...], preferred_element_type=F32)
    out_ref[...] = _layer_norm(y, g_ref[...], b_ref[...])


def _oproj_ln(x, o, wo, g, b, *, tm=512):
    t, d = x.shape
    row = pl.BlockSpec((tm, d), lambda i: (i, 0))
    vec = pl.BlockSpec((1, d), lambda i: (0, 0))
    return pl.pallas_call(
        _oproj_kernel,
        grid=(t // tm,),
        in_specs=[row, row, pl.BlockSpec((d, d), lambda i: (0, 0)), vec, vec],
        out_specs=row,
        out_shape=jax.ShapeDtypeStruct((t, d), F32),
        compiler_params=_params(("parallel",)),
        name="oproj_ln",
    )(x, o, wo, g, b)


def kernel(x_prompt, x_sample, state_pool, cache_fox_k, cache_fox_v, cache_fox_logf, ln_g, ln_b,
           ffn_w1, ffn_w3, ffn_w2, pool_w, pool_scale, fox_w_in, fox_b_f, fox_w_o):
    d = D_MODEL
    _, seq, _ = x_prompt.shape
    nb, tnew, _ = x_sample.shape
    past = cache_fox_k.shape[2]
    ns = nb * tnew

    def ffn(xp, xs, i, s, ln_idx):
        g, b = ln_g[i, ln_idx][None], ln_b[i, ln_idx][None]
        xs, w1b, w3b, w2b = _ffn_ln(xs, ffn_w1, ffn_w3, ffn_w2, g, b, tm=ns, tf=256, layer_half=(i, s))
        return _ffn_ln(xp, w1b, w3b, w2b, g, b, tm=512, tf=1024), xs

    xp, xs = x_prompt.reshape(seq, d), x_sample.reshape(ns, d)

    xp, xs = ffn(xp, xs, 0, 0, 0)
    xp3, xs3 = xp.reshape(1, seq, d), xs.reshape(nb, tnew, d)
    pool_prompt = xp3[:, seq - POOL_STATE:][None]
    pool_sample = jnp.concatenate([state_pool[0], xs3], axis=1)[:, -POOL_STATE:][None]
    pw = pool_w[0].astype(BF16)
    ps, g1, b1 = pool_scale[0][None], ln_g[0, 1][None], ln_b[0, 1][None]
    tm = 512
    xp = _pool_ln(xp3, xp3, lambda bi, i: (bi, jnp.maximum(i * (tm // HALO) - 1, 0), 0), pw, ps, g1, b1,
                  bb=1, tm=tm, start_pos=0, zero_first=True).reshape(seq, d)
    prev = jnp.pad(state_pool[0], ((0, 0), (HALO - POOL_STATE, 0), (0, 0)))
    xs = _pool_ln(xs3, prev, lambda bi, i: (bi, 0, 0), pw, ps, g1, b1,
                  bb=nb, tm=tnew, start_pos=past, zero_first=False).reshape(ns, d)
    xp, xs = ffn(xp, xs, 0, 1, 2)

    xp, xs = ffn(xp, xs, 1, 0, 0)
    w_in = fox_w_in[0].astype(BF16)
    wf = jnp.pad(w_in[:, 3 * d:], ((0, 0), (0, LANES - HEADS)))
    bf = jnp.pad(fox_b_f[0], (0, LANES - HEADS))[None]
    q_p, k_p, v_p, kb_p, vb_p, lf_p, qmax, kmax = _fox_proj(xp, w_in, wf, bf)
    q_s, k_s, v_s, kb_s, vb_s, lf_s, _, _ = _fox_proj(xs, w_in, wf, bf)

    c_p = _cumsum_time(lf_p[None], tc=512)[0]
    o_p = _attn_prompt(q_p, kb_p, vb_p, c_p, qmax[:, 0, :HEADS], kmax[:, 0, :HEADS])

    total = past + tnew
    lf_all = jnp.concatenate([cache_fox_logf[0], lf_s[:, :HEADS].reshape(nb, tnew, HEADS)], axis=1)
    c_s = _cumsum_time(lf_all.transpose(1, 0, 2).reshape(1, total, nb * HEADS), tc=total // 3)
    c_s = c_s.reshape(total, nb, HEADS)
    ct_s = c_s.transpose(1, 2, 0)
    c_new = jnp.pad(c_s[past:].transpose(1, 0, 2).reshape(ns, HEADS), ((0, 0), (0, LANES - HEADS)))
    o_s = _attn_sample(q_s, cache_fox_k[0].reshape(nb, past * HEADS, HEAD_DIM),
                       cache_fox_v[0].reshape(nb, past * HEADS, HEAD_DIM), kb_s, vb_s,
                       c_new, ct_s[:, :, :past], ct_s[:, :, past:], tnew=tnew)

    wo = fox_w_o[0].astype(BF16)
    g1, b1 = ln_g[1, 1][None], ln_b[1, 1][None]
    xp, xs = _oproj_ln(xp, o_p, wo, g1, b1), _oproj_ln(xs, o_s, wo, g1, b1)
    xp, xs = ffn(xp, xs, 1, 1, 2)

    shp = (HEADS, HEAD_DIM)
    return (xp.reshape(1, seq, d), xs.reshape(nb, tnew, d), pool_prompt, pool_sample,
            k_p.reshape(1, 1, seq, *shp), v_p.reshape(1, 1, seq, *shp), lf_p[:, :HEADS].reshape(1, 1, seq, HEADS),
            k_s.reshape(1, nb, tnew, *shp), v_s.reshape(1, nb, tnew, *shp),
            lf_s[:, :HEADS].reshape(1, nb, tnew, HEADS))
```

```python
import functools
import math

import jax
import jax.numpy as jnp
import numpy as np
from jax import lax
from jax.experimental import pallas as pl
from jax.experimental.pallas import tpu as pltpu

F32 = jnp.float32
BF16 = jnp.bfloat16

D_MODEL = 2048
DEPTH = 2
POOL_WINDOWS = (2, 4, 8, 16)
POOL_GROUP = D_MODEL // len(POOL_WINDOWS)
POOL_STATE = max(POOL_WINDOWS) - 1
SUBLANES = 8
HALO = SUBLANES * len(POOL_WINDOWS)
HEAD_DIM = 128
HEADS = D_MODEL // HEAD_DIM
LN_EPS = 1e-5
DN_ALPHA = (2 * DEPTH) ** 0.25
LOG2E = math.log2(math.e)
Q_SCALE = HEAD_DIM ** -0.5 * LOG2E
NEG_BIG = -1e30
SKIP_LOG2 = 150.0
LANES = 128
VMEM_LIMIT = 60 * 1024 * 1024


def _params(semantics):
    return pltpu.CompilerParams(dimension_semantics=semantics, vmem_limit_bytes=VMEM_LIMIT)


def _layer_norm(y, g, b):
    mu = jnp.mean(y, axis=-1, keepdims=True)
    yc = y - mu
    var = jnp.mean(yc * yc, axis=-1, keepdims=True)
    return yc * lax.rsqrt(var + LN_EPS) * g + b


def _ffn_kernel(x_ref, w1_ref, w3_ref, w2_ref, g_ref, b_ref, o_ref, *rest):
    *wb_refs, xb_ref = rest
    j = pl.program_id(1)

    def partial_sum():
        w1, w3, w2 = w1_ref[...].astype(BF16), w3_ref[...].astype(BF16), w2_ref[...].astype(BF16)
        for wb_ref, w in zip(wb_refs, (w1, w3, w2)):
            wb_ref[...] = w
        xb = xb_ref[...]
        h1 = jnp.dot(xb, w1, preferred_element_type=F32)
        h3 = jnp.dot(xb, w3, preferred_element_type=F32)
        gate = (h1 * jax.nn.sigmoid(h1) * h3).astype(BF16)
        return jnp.dot(gate, w2, preferred_element_type=F32)

    @pl.when(j == 0)
    def _():
        x = x_ref[...]
        xb_ref[...] = x.astype(BF16)
        o_ref[...] = (2.0 * DN_ALPHA) * x + partial_sum()

    @pl.when(j > 0)
    def _():
        o_ref[...] += partial_sum()

    @pl.when(j == pl.num_programs(1) - 1)
    def _():
        o_ref[...] = _layer_norm(0.5 * o_ref[...], g_ref[...], b_ref[...])


def _ffn_ln(x, w1, w3, w2, g, b, *, tm, tf, layer_half=None):
    t, d = x.shape
    f = w1.shape[-1]
    emit = layer_half is not None
    assert not emit or t == tm

    def wspec(block, idx):
        if emit:
            return pl.BlockSpec((None, None) + block, lambda i, j: layer_half + idx(j))
        return pl.BlockSpec(block, lambda i, j: idx(j))

    up, down = wspec((d, tf), lambda j: (0, j)), wspec((tf, d), lambda j: (j, 0))
    out_specs = [pl.BlockSpec((tm, d), lambda i, j: (i, 0))]
    out_shape = [jax.ShapeDtypeStruct((t, d), F32)]
    if emit:
        out_specs += [pl.BlockSpec((d, tf), lambda i, j: (0, j)), pl.BlockSpec((d, tf), lambda i, j: (0, j)),
                      pl.BlockSpec((tf, d), lambda i, j: (j, 0))]
        out_shape += [jax.ShapeDtypeStruct((d, f), BF16), jax.ShapeDtypeStruct((d, f), BF16),
                      jax.ShapeDtypeStruct((f, d), BF16)]
    res = pl.pallas_call(
        _ffn_kernel,
        grid=(t // tm, f // tf),
        in_specs=[
            pl.BlockSpec((tm, d), lambda i, j: (i, 0)),
            up, up, down,
            pl.BlockSpec((1, d), lambda i, j: (0, 0)),
            pl.BlockSpec((1, d), lambda i, j: (0, 0)),
        ],
        out_specs=out_specs,
        out_shape=out_shape,
        scratch_shapes=[pltpu.VMEM((tm, d), BF16)],
        compiler_params=_params(("parallel", "arbitrary")),
        name="ffn_ln",
    )(x, w1, w3, w2, g, b)
    return res if emit else res[0]


def _pool_kernel(x_ref, halo_ref, pw_ref, ps_ref, g_ref, b_ref, o_ref, buf_ref, *, tm, start_pos, zero_first):
    i = pl.program_id(1)
    bb = x_ref.shape[0]
    total = HALO + tm
    x = x_ref[...]
    halo = halo_ref[...]
    if zero_first:
        halo = jnp.where(i == 0, 0.0, halo)
    pos = start_pos + i * tm + lax.broadcasted_iota(jnp.int32, (1, tm, 1), 1)
    ys = []
    for g, w in enumerate(POOL_WINDOWS):
        cols = slice(g * POOL_GROUP, (g + 1) * POOL_GROUP)
        xg = x[:, :, cols]
        buf_ref[0, :, 0:HALO, :] = halo[:, :, cols]
        buf_ref[0, :, HALO:, :] = xg
        src = 0
        for s in range(g + 1):
            lo, back = SUBLANES * (s + 1), 1 << s
            level = buf_ref[src, :, lo:total, :] + buf_ref[src, :, lo - back:total - back, :]
            if s < g:
                buf_ref[1 - src, :, lo:total, :] = level
                src = 1 - src
        win = level[:, HALO - lo:, :]
        cnt = jnp.minimum(pos + 1, w).astype(F32)
        diff = (win / cnt - xg).reshape(bb * tm, POOL_GROUP)
        ys.append(jnp.dot(diff.astype(BF16), pw_ref[g], preferred_element_type=F32))
    y = jnp.concatenate(ys, axis=-1) * ps_ref[...]
    out = _layer_norm(DN_ALPHA * x.reshape(bb * tm, D_MODEL) + y, g_ref[...], b_ref[...])
    o_ref[...] = out.reshape(bb, tm, D_MODEL)


def _pool_ln(x, halo_src, halo_map, pw, ps, g, b, *, bb, tm, start_pos, zero_first):
    nb, t, d = x.shape
    kern = functools.partial(_pool_kernel, tm=tm, start_pos=start_pos, zero_first=zero_first)
    return pl.pallas_call(
        kern,
        grid=(nb // bb, t // tm),
        in_specs=[
            pl.BlockSpec((bb, tm, d), lambda bi, i: (bi, i, 0)),
            pl.BlockSpec((bb, HALO, d), halo_map),
            pl.BlockSpec((len(POOL_WINDOWS), POOL_GROUP, POOL_GROUP), lambda bi, i: (0, 0, 0)),
            pl.BlockSpec((1, d), lambda bi, i: (0, 0)),
            pl.BlockSpec((1, d), lambda bi, i: (0, 0)),
            pl.BlockSpec((1, d), lambda bi, i: (0, 0)),
        ],
        out_specs=pl.BlockSpec((bb, tm, d), lambda bi, i: (bi, i, 0)),
        out_shape=jax.ShapeDtypeStruct((nb, t, d), F32),
        scratch_shapes=[pltpu.VMEM((2, bb, HALO + tm, POOL_GROUP), F32)],
        compiler_params=_params(("parallel", "arbitrary")),
        name="pool_ln",
    )(x, halo_src, pw, ps, g, b)


def _max_sq_norm_per_head(xb, first_head, acc):
    sq = xb.astype(F32)
    sq = sq * sq
    lane = lax.broadcasted_iota(jnp.int32, acc.shape, 1)
    for hh in range(xb.shape[1] // HEAD_DIM):
        row = jnp.sum(sq[:, hh * HEAD_DIM:(hh + 1) * HEAD_DIM], axis=-1, keepdims=True)
        acc = jnp.where(lane == first_head + hh, jnp.max(row, axis=0, keepdims=True), acc)
    return acc


def _proj_kernel(x_ref, wq_ref, wk_ref, wv_ref, wf_ref, bf_ref,
                 q_ref, k_ref, v_ref, kb_ref, vb_ref, lf_ref, qn_ref, kn_ref, xb_ref):
    j = pl.program_id(1)
    heads_per_step = q_ref.shape[1] // HEAD_DIM

    @pl.when(j == 0)
    def _():
        xb = x_ref[...].astype(BF16)
        xb_ref[...] = xb
        fl = jnp.dot(xb, wf_ref[...], preferred_element_type=F32) + bf_ref[...]
        lf_ref[...] = jnp.minimum(fl, 0.0) - jnp.log1p(jnp.exp(-jnp.abs(fl)))
        qn_ref[...] = jnp.zeros_like(qn_ref)
        kn_ref[...] = jnp.zeros_like(kn_ref)

    xb = xb_ref[...]
    qb = (jnp.dot(xb, wq_ref[...], preferred_element_type=F32) * Q_SCALE).astype(BF16)
    q_ref[...] = qb
    k = jnp.dot(xb, wk_ref[...], preferred_element_type=F32)
    kb = k.astype(BF16)
    kb_ref[...] = kb
    v = jnp.dot(xb, wv_ref[...], preferred_element_type=F32)
    vb_ref[...] = v.astype(BF16)
    tm = x_ref.shape[0]
    for hh in range(heads_per_step):
        rows = pl.ds(j * heads_per_step + hh, tm, stride=HEADS)
        k_ref[rows, :] = k[:, hh * HEAD_DIM:(hh + 1) * HEAD_DIM]
        v_ref[rows, :] = v[:, hh * HEAD_DIM:(hh + 1) * HEAD_DIM]
    qn_ref[...] = _max_sq_norm_per_head(qb, j * heads_per_step, qn_ref[...])
    kn_ref[...] = _max_sq_norm_per_head(kb, j * heads_per_step, kn_ref[...])


def _fox_proj(x, w_in, wf, bf, *, tm=512, tn=512):
    t, d = x.shape
    nd = d // tn
    row = pl.BlockSpec((tm, tn), lambda i, j: (i, j))

    def wcols(part):
        return pl.BlockSpec((d, tn), lambda i, j: (0, part * nd + j))

    wq, wk, wv = wcols(0), wcols(1), wcols(2)
    tile_stat = pl.BlockSpec((None, SUBLANES, LANES), lambda i, j: (i, 0, 0))
    by_head = pl.BlockSpec((tm * HEADS, HEAD_DIM), lambda i, j: (i, 0))
    return pl.pallas_call(
        _proj_kernel,
        grid=(t // tm, d // tn),
        in_specs=[
            pl.BlockSpec((tm, d), lambda i, j: (i, 0)),
            wq, wk, wv,
            pl.BlockSpec((d, LANES), lambda i, j: (0, 0)),
            pl.BlockSpec((1, LANES), lambda i, j: (0, 0)),
        ],
        out_specs=[row, by_head, by_head, row, row, pl.BlockSpec((tm, LANES), lambda i, j: (i, 0)), tile_stat, tile_stat],
        out_shape=[
            jax.ShapeDtypeStruct((t, d), BF16),
            jax.ShapeDtypeStruct((t * HEADS, HEAD_DIM), F32),
            jax.ShapeDtypeStruct((t * HEADS, HEAD_DIM), F32),
            jax.ShapeDtypeStruct((t, d), BF16),
            jax.ShapeDtypeStruct((t, d), BF16),
            jax.ShapeDtypeStruct((t, LANES), F32),
            jax.ShapeDtypeStruct((t // tm, SUBLANES, LANES), F32),
            jax.ShapeDtypeStruct((t // tm, SUBLANES, LANES), F32),
        ],
        scratch_shapes=[pltpu.VMEM((tm, d), BF16)],
        compiler_params=_params(("parallel", "arbitrary")),
        name="fox_proj",
    )(x, w_in, w_in, w_in, wf, bf)


def _split3(x):
    hi = x.astype(BF16)
    rem = x - hi.astype(F32)
    mid = rem.astype(BF16)
    lo = (rem - mid.astype(F32)).astype(BF16)
    return hi, mid, lo


def _cumsum_kernel(x_ref, o_ref, carry_ref, *, tc):
    @pl.when(pl.program_id(1) == 0)
    def _():
        carry_ref[...] = jnp.zeros_like(carry_ref)

    r = lax.broadcasted_iota(jnp.int32, (tc, tc), 0)
    c = lax.broadcasted_iota(jnp.int32, (tc, tc), 1)
    tri = (c <= r).astype(BF16)
    hi, mid, lo = _split3(x_ref[...])
    cs = (jnp.dot(tri, hi, preferred_element_type=F32)
          + jnp.dot(tri, mid, preferred_element_type=F32)
          + jnp.dot(tri, lo, preferred_element_type=F32)) + carry_ref[...]
    o_ref[...] = cs * LOG2E
    carry_ref[...] = cs[tc - 1:tc, :]


def _cumsum_time(x, *, tc):
    nb, t, w = x.shape
    return pl.pallas_call(
        functools.partial(_cumsum_kernel, tc=tc),
        grid=(nb, t // tc),
        in_specs=[pl.BlockSpec((None, tc, w), lambda b, j: (b, j, 0))],
        out_specs=pl.BlockSpec((None, tc, w), lambda b, j: (b, j, 0)),
        out_shape=jax.ShapeDtypeStruct((nb, t, w), F32),
        scratch_shapes=[pltpu.VMEM((1, w), F32)],
        compiler_params=_params(("parallel", "arbitrary")),
        name="cumsum_time",
    )(x)


N_SPLIT = 3
GROUP = LANES // HEADS
assert 2 * N_SPLIT <= GROUP
QK_AHEAD = 1


def _bias_selectors():
    sel_q = np.zeros((N_SPLIT * LANES, LANES), np.float32)
    sel_k = np.zeros((N_SPLIT * LANES, LANES), np.float32)
    one_q = np.zeros((1, LANES), np.float32)
    one_k = np.zeros((1, LANES), np.float32)
    for h in range(HEADS):
        for part in range(N_SPLIT):
            sel_q[part * LANES + h, h * GROUP + part] = 1.0
            sel_k[part * LANES + h, h * GROUP + N_SPLIT + part] = -1.0
            one_q[0, h * GROUP + N_SPLIT + part] = 1.0
            one_k[0, h * GROUP + part] = 1.0
    return jnp.asarray(sel_q, BF16), jnp.asarray(sel_k, BF16), jnp.asarray(one_q), jnp.asarray(one_k)


def _bias_operands_kernel(c_ref, sel_q_ref, sel_k_ref, one_q_ref, one_k_ref, a_ref, b_ref):
    parts = jnp.concatenate(_split3(c_ref[...]), axis=1)
    a_ref[...] = (jnp.dot(parts, sel_q_ref[...], preferred_element_type=F32) + one_q_ref[...]).astype(BF16)
    b_ref[...] = (jnp.dot(parts, sel_k_ref[...], preferred_element_type=F32) + one_k_ref[...]).astype(BF16)


def _bias_operands(c2, *, tr=2048):
    s, w = c2.shape
    rows = pl.BlockSpec((tr, w), lambda i: (i, 0))
    sel = pl.BlockSpec((N_SPLIT * w, w), lambda i: (0, 0))
    one = pl.BlockSpec((1, w), lambda i: (0, 0))
    return pl.pallas_call(
        _bias_operands_kernel,
        grid=(s // tr,),
        in_specs=[rows, sel, sel, one, one],
        out_specs=[rows, rows],
        out_shape=[jax.ShapeDtypeStruct((s, w), BF16), jax.ShapeDtypeStruct((s, w), BF16)],
        compiler_params=_params(("parallel",)),
        name="bias_operands",
    )(c2, *_bias_selectors())


def _qk(q_h, k_h):
    return lax.dot_general(q_h, k_h, (((1,), (1,)), ((), ())), preferred_element_type=F32)


def _attn_schedule(qmax, kmax, c2, tq, nsteps):
    nq = qmax.shape[0]
    qn, kn = jnp.sqrt(qmax), jnp.sqrt(kmax)
    c_first, c_last = c2[0::tq, :HEADS], c2[tq - 1::tq, :HEADS]
    bound = qn[:, None, :] * (kn[None, :, :] + kn[:, None, :]) + c_first[:, None, :] - c_last[None, :, :]
    tile = jnp.arange(nq, dtype=jnp.int32)
    skip = jnp.all(bound < -SKIP_LOG2, axis=-1) & (tile[None, :] < tile[:, None])
    jstart = jnp.argmin(skip.astype(jnp.int32), axis=1).astype(jnp.int32)
    cnt = tile - jstart + 1
    ends = jnp.cumsum(cnt)
    starts = ends - cnt
    n = jnp.arange(nsteps, dtype=jnp.int32)
    valid = n < ends[-1]
    qi = jnp.minimum(jnp.sum((ends[None, :] <= n[:, None]).astype(jnp.int32), axis=1), nq - 1)
    kj = jstart[qi] + n - starts[qi]
    qi = jnp.where(valid, qi, nq - 1)
    kj = jnp.where(valid, kj, nq - 1)
    first = valid & (kj == jstart[qi])
    return qi, kj, valid.astype(jnp.int32) + 2 * first.astype(jnp.int32)


def _attn_prompt_kernel(qi_ref, kj_ref, fl_ref, q_ref, qc_ref, k_ref, kc_ref, v_ref, o_ref,
                        m_ref, l_ref, acc_ref, qch_ref, *, tq):
    n = pl.program_id(0)
    flags = fl_ref[n]
    nchunk = tq // LANES

    @pl.when(flags >= 2)
    def _():
        m_ref[...] = jnp.full_like(m_ref, NEG_BIG)
        l_ref[...] = jnp.zeros_like(l_ref)
        acc_ref[...] = jnp.zeros_like(acc_ref)
        qc = qc_ref[...]
        group = lax.broadcasted_iota(jnp.int32, qc.shape, 1) // GROUP
        for h in range(HEADS):
            qch_ref[h] = jnp.where(group == h, qc, jnp.zeros_like(qc))

    def sweep(masked):
        if masked:
            row = lax.broadcasted_iota(jnp.int32, (tq, LANES), 0)
            lane = lax.broadcasted_iota(jnp.int32, (tq, LANES), 1)

        def logits(h):
            sl = slice(h * HEAD_DIM, (h + 1) * HEAD_DIM)
            return _qk(jnp.concatenate([q_ref[:, sl], qch_ref[h]], axis=1),
                       jnp.concatenate([k_ref[:, sl], kc_ref[...]], axis=1))

        pending = [logits(h) for h in range(QK_AHEAD)]
        for h in range(HEADS):
            sl = slice(h * HEAD_DIM, (h + 1) * HEAD_DIM)
            if h + QK_AHEAD < HEADS:
                pending.append(logits(h + QK_AHEAD))
            s = pending.pop(0)
            chunks = []
            for c in range(nchunk):
                sc = s[:, c * LANES:(c + 1) * LANES]
                if masked:
                    sc = jnp.where(lane + c * LANES <= row, sc, NEG_BIG)
                chunks.append(sc)
            mx = chunks[0]
            for sc in chunks[1:]:
                mx = jnp.maximum(mx, sc)
            m_prev = m_ref[h]
            m_new = jnp.maximum(m_prev, jnp.max(mx, axis=-1, keepdims=True))
            a = jnp.exp2(m_prev - m_new)
            ps = [jnp.exp2(sc - m_new) for sc in chunks]
            lsum = ps[0]
            for p in ps[1:]:
                lsum = lsum + p
            l_ref[h] = a * l_ref[h] + lsum
            p = jnp.concatenate(ps, axis=-1).astype(BF16)
            acc_ref[:, sl] = a * acc_ref[:, sl] + jnp.dot(p, v_ref[:, sl], preferred_element_type=F32)
            m_ref[h] = m_new

    diagonal = kj_ref[n] == qi_ref[n]

    @pl.when((flags >= 1) & jnp.logical_not(diagonal))
    def _():
        sweep(False)

    @pl.when((flags >= 1) & diagonal)
    def _():
        sweep(True)
        for h in range(HEADS):
            sl = slice(h * HEAD_DIM, (h + 1) * HEAD_DIM)
            l_tot = jnp.sum(l_ref[h], axis=-1, keepdims=True)
            o_ref[:, sl] = (acc_ref[:, sl] / l_tot).astype(BF16)


def _attn_prompt(q, kb, vb, c2, qmax, kmax, *, tq=512):
    s, d = q.shape
    nq = s // tq
    nsteps = nq * (nq + 1) // 2
    qi, kj, flags = _attn_schedule(qmax, kmax, c2, tq, nsteps)
    qc, kc = _bias_operands(c2)
    qrow = pl.BlockSpec((tq, d), lambda n, qi, kj, fl: (qi[n], 0))
    krow = pl.BlockSpec((tq, d), lambda n, qi, kj, fl: (kj[n], 0))
    qext = pl.BlockSpec((tq, LANES), lambda n, qi, kj, fl: (qi[n], 0))
    kext = pl.BlockSpec((tq, LANES), lambda n, qi, kj, fl: (kj[n], 0))
    stat = pltpu.VMEM((HEADS, tq, LANES), F32)
    return pl.pallas_call(
        functools.partial(_attn_prompt_kernel, tq=tq),
        grid_spec=pltpu.PrefetchScalarGridSpec(
            num_scalar_prefetch=3,
            grid=(nsteps,),
            in_specs=[qrow, qext, krow, kext, krow],
            out_specs=qrow,
            scratch_shapes=[stat, stat, pltpu.VMEM((tq, d), F32), pltpu.VMEM((HEADS, tq, LANES), BF16)],
        ),
        out_shape=jax.ShapeDtypeStruct((s, d), BF16),
        compiler_params=_params(("arbitrary",)),
        name="attn_prompt",
    )(qi, kj, flags, q, qc, kb, kc, vb)


def _attn_sample_kernel(q_ref, kc_ref, vc_ref, kn_ref, vn_ref, cq_ref, ctc_ref, ctn_ref, o_ref,
                        m_ref, l_ref, acc_ref, cqr_ref, *, tp):
    j = pl.program_id(1)
    tnew = q_ref.shape[0]
    heads = [slice(h * HEAD_DIM, (h + 1) * HEAD_DIM) for h in range(HEADS)]

    @pl.when(j == 0)
    def _():
        m_ref[...] = jnp.full_like(m_ref, NEG_BIG)
        l_ref[...] = jnp.zeros_like(l_ref)
        acc_ref[...] = jnp.zeros_like(acc_ref)
        for h in range(HEADS):
            cqr_ref[h * tnew:(h + 1) * tnew, :] = jnp.broadcast_to(cq_ref[:, h:h + 1], (tnew, LANES))

    def update(keys, values, key_c2, visible):
        n = key_c2.shape[1]
        s = jnp.concatenate([_qk(q_ref[:, heads[h]], keys[h]) for h in range(HEADS)], axis=0)
        c_keys = jnp.concatenate([jnp.broadcast_to(key_c2[h:h + 1, :], (tnew, n)) for h in range(HEADS)], axis=0)
        s = s + cqr_ref[:, :1] - c_keys
        if visible is not None:
            s = jnp.where(visible, s, NEG_BIG)
        m_prev = m_ref[...]
        m_new = jnp.maximum(m_prev, jnp.max(s, axis=-1, keepdims=True))
        a = jnp.exp2(m_prev - m_new)
        p = jnp.exp2(s - m_new[:, :1])
        l_ref[...] = a * l_ref[...] + jnp.sum(p, axis=-1, keepdims=True)
        pb = p.astype(BF16)
        pv = jnp.concatenate([jnp.dot(pb[h * tnew:(h + 1) * tnew], values[h], preferred_element_type=F32)
                              for h in range(HEADS)], axis=0)
        acc_ref[...] = a * acc_ref[...] + pv
        m_ref[...] = m_new

    update([kc_ref[pl.ds(h, tp, stride=HEADS), :].astype(BF16) for h in range(HEADS)],
           [vc_ref[pl.ds(h, tp, stride=HEADS), :].astype(BF16) for h in range(HEADS)],
           ctc_ref[...], None)

    @pl.when(j == pl.num_programs(1) - 1)
    def _():
        query = lax.broadcasted_iota(jnp.int32, (HEADS * tnew, tnew), 0) % tnew
        key = lax.broadcasted_iota(jnp.int32, (HEADS * tnew, tnew), 1)
        update([kn_ref[:, sl] for sl in heads], [vn_ref[:, sl] for sl in heads], ctn_ref[...], key <= query)
        out = acc_ref[...] / l_ref[...]
        for h in range(HEADS):
            o_ref[:, heads[h]] = out[h * tnew:(h + 1) * tnew, :].astype(BF16)


def _attn_sample(q, k_cache, v_cache, kb_new, vb_new, c_new, ct_cache, ct_new, *, tnew, tp=512):
    t, d = q.shape
    nb, past_rows, _ = k_cache.shape
    new_rows = pl.BlockSpec((tnew, d), lambda b, j: (b, 0))
    cache = pl.BlockSpec((None, tp * HEADS, HEAD_DIM), lambda b, j: (b, j, 0))
    return pl.pallas_call(
        functools.partial(_attn_sample_kernel, tp=tp),
        grid=(nb, past_rows // (tp * HEADS)),
        in_specs=[
            new_rows, cache, cache, new_rows, new_rows,
            pl.BlockSpec((tnew, LANES), lambda b, j: (b, 0)),
            pl.BlockSpec((None, HEADS, tp), lambda b, j: (b, 0, j)),
            pl.BlockSpec((None, HEADS, tnew), lambda b, j: (b, 0, 0)),
        ],
        out_specs=new_rows,
        out_shape=jax.ShapeDtypeStruct((t, d), BF16),
        scratch_shapes=[pltpu.VMEM((HEADS * tnew, LANES), F32), pltpu.VMEM((HEADS * tnew, LANES), F32),
                        pltpu.VMEM((HEADS * tnew, HEAD_DIM), F32), pltpu.VMEM((HEADS * tnew, LANES), F32)],
        compiler_params=_params(("parallel", "arbitrary")),
        name="attn_sample",
    )(q, k_cache, v_cache, kb_new, vb_new, c_new, ct_cache, ct_new)


def _oproj_kernel(x_ref, o_ref, wo_ref, g_ref, b_ref, out_ref):
    y = DN_ALPHA * x_ref[...] + jnp.dot(o_ref[...], wo_ref[...], preferred_element_type=F32)
    out_ref[...] = _layer_norm(y, g_ref[...], b_ref[...])


def _oproj_ln(x, o, wo, g, b, *, tm=512):
    t, d = x.shape
    row = pl.BlockSpec((tm, d), lambda i: (i, 0))
    vec = pl.BlockSpec((1, d), lambda i: (0, 0))
    return pl.pallas_call(
        _oproj_kernel,
        grid=(t // tm,),
        in_specs=[row, row, pl.BlockSpec((d, d), lambda i: (0, 0)), vec, vec],
        out_specs=row,
        out_shape=jax.ShapeDtypeStruct((t, d), F32),
        compiler_params=_params(("parallel",)),
        name="oproj_ln",
    )(x, o, wo, g, b)


def kernel(x_prompt, x_sample, state_pool, cache_fox_k, cache_fox_v, cache_fox_logf, ln_g, ln_b,
           ffn_w1, ffn_w3, ffn_w2, pool_w, pool_scale, fox_w_in, fox_b_f, fox_w_o):
    d = D_MODEL
    _, seq, _ = x_prompt.shape
    nb, tnew, _ = x_sample.shape
    past = cache_fox_k.shape[2]
    ns = nb * tnew

    def ffn(xp, xs, i, s, ln_idx):
        g, b = ln_g[i, ln_idx][None], ln_b[i, ln_idx][None]
        xs, w1b, w3b, w2b = _ffn_ln(xs, ffn_w1, ffn_w3, ffn_w2, g, b, tm=ns, tf=512, layer_half=(i, s))
        return _ffn_ln(xp, w1b, w3b, w2b, g, b, tm=512, tf=1024), xs

    xp, xs = x_prompt.reshape(seq, d), x_sample.reshape(ns, d)

    xp, xs = ffn(xp, xs, 0, 0, 0)
    xp3, xs3 = xp.reshape(1, seq, d), xs.reshape(nb, tnew, d)
    pool_prompt = xp3[:, seq - POOL_STATE:][None]
    pool_sample = jnp.concatenate([state_pool[0], xs3], axis=1)[:, -POOL_STATE:][None]
    pw = pool_w[0].astype(BF16)
    ps, g1, b1 = pool_scale[0][None], ln_g[0, 1][None], ln_b[0, 1][None]
    tm = 512
    xp = _pool_ln(xp3, xp3, lambda bi, i: (bi, jnp.maximum(i * (tm // HALO) - 1, 0), 0), pw, ps, g1, b1,
                  bb=1, tm=tm, start_pos=0, zero_first=True).reshape(seq, d)
    prev = jnp.pad(state_pool[0], ((0, 0), (HALO - POOL_STATE, 0), (0, 0)))
    xs = _pool_ln(xs3, prev, lambda bi, i: (bi, 0, 0), pw, ps, g1, b1,
                  bb=nb, tm=tnew, start_pos=past, zero_first=False).reshape(ns, d)
    xp, xs = ffn(xp, xs, 0, 1, 2)

    xp, xs = ffn(xp, xs, 1, 0, 0)
    w_in = fox_w_in[0].astype(BF16)
    wf = jnp.pad(w_in[:, 3 * d:], ((0, 0), (0, LANES - HEADS)))
    bf = jnp.pad(fox_b_f[0], (0, LANES - HEADS))[None]
    q_p, k_p, v_p, kb_p, vb_p, lf_p, qmax, kmax = _fox_proj(xp, w_in, wf, bf)
    q_s, k_s, v_s, kb_s, vb_s, lf_s, _, _ = _fox_proj(xs, w_in, wf, bf)

    c_p = _cumsum_time(lf_p[None], tc=512)[0]
    o_p = _attn_prompt(q_p, kb_p, vb_p, c_p, qmax[:, 0, :HEADS], kmax[:, 0, :HEADS])

    total = past + tnew
    lf_all = jnp.concatenate([cache_fox_logf[0], lf_s[:, :HEADS].reshape(nb, tnew, HEADS)], axis=1)
    c_s = _cumsum_time(lf_all.transpose(1, 0, 2).reshape(1, total, nb * HEADS), tc=total // 3)
    c_s = c_s.reshape(total, nb, HEADS)
    ct_s = c_s.transpose(1, 2, 0)
    c_new = jnp.pad(c_s[past:].transpose(1, 0, 2).reshape(ns, HEADS), ((0, 0), (0, LANES - HEADS)))
    o_s = _attn_sample(q_s, cache_fox_k[0].reshape(nb, past * HEADS, HEAD_DIM),
                       cache_fox_v[0].reshape(nb, past * HEADS, HEAD_DIM), kb_s, vb_s,
                       c_new, ct_s[:, :, :past], ct_s[:, :, past:], tnew=tnew)

    wo = fox_w_o[0].astype(BF16)
    g1, b1 = ln_g[1, 1][None], ln_b[1, 1][None]
    xp, xs = _oproj_ln(xp, o_p, wo, g1, b1), _oproj_ln(xs, o_s, wo, g1, b1)
    xp, xs = ffn(xp, xs, 1, 1, 2)

    shp = (HEADS, HEAD_DIM)
    return (xp.reshape(1, seq, d), xs.reshape(nb, tnew, d), pool_prompt, pool_sample,
            k_p.reshape(1, 1, seq, *shp), v_p.reshape(1, 1, seq, *shp), lf_p[:, :HEADS].reshape(1, 1, seq, HEADS),
            k_s.reshape(1, nb, tnew, *shp), v_s.reshape(1, nb, tnew, *shp),
            lf_s[:, :HEADS].reshape(1, nb, tnew, HEADS))
```
